```python
import jax, jax.numpy as jnp
from jax import lax
import numpy as np

D_MODEL = 2048
BATCH = 16
SEQ = 2048
DEPTH = 2
DEC_BATCH = 32
DEC_SEQ = 16
PAST_LEN = 4096

CHUNK = 64
N_MIXERS = 2
N_SB_LAYERS = (DEPTH + 1) // 2
N_POOL_LAYERS = DEPTH // 2
N_HEADS = 16
HEAD_DIM = D_MODEL // N_HEADS
Q_BLOCK = 128
POOL_WINDOWS = (2, 4, 8, 16)
N_POOL_GROUPS = len(POOL_WINDOWS)
POOL_GROUP = D_MODEL // N_POOL_GROUPS
POOL_STATE = max(POOL_WINDOWS) - 1
D_FF = -(-8 * D_MODEL // (3 * 256)) * 256
N_MOD = 6
EPS = 1e-6

kernel_name = 'stickbreak_pool_streaming_encoder_step'


def rms_norm(x, g):
    xf = x.astype(jnp.float32)
    y = xf * lax.rsqrt(jnp.mean(xf * xf, axis=-1, keepdims=True) + EPS)
    return (y * g.astype(jnp.float32)).astype(x.dtype)


def ada_modulation(c, w, b):
    mod = jax.nn.silu(c) @ w + b
    return [m[:, None, :] for m in jnp.split(mod, N_MOD, axis=-1)]


def qkv_project(h, w_qkv, g_q, g_k):
    b, t, _ = h.shape
    qkv = (h @ w_qkv).reshape(b, t, 3, N_HEADS, HEAD_DIM)
    q = rms_norm(qkv[:, :, 0], g_q)
    k = rms_norm(qkv[:, :, 1], g_k)
    return q, k, qkv[:, :, 2]


def stick_breaking(q, k, v, q_pos, k_pos):
    z = jnp.einsum('bqhd,bkhd->bhqk', q, k, preferred_element_type=jnp.float32) * (HEAD_DIM ** -0.5)
    causal = k_pos[None, :] < q_pos[:, None]
    ls_neg = jax.nn.log_sigmoid(-z)
    log_1m = jnp.where(causal, ls_neg, 0.0)
    later = lax.cumsum(log_1m, axis=3, reverse=True) - log_1m
    weights = jnp.where(causal, jnp.exp(z + ls_neg + later), 0.0)
    return jnp.einsum('bhqk,bkhd->bqhd', weights.astype(v.dtype), v)


def sb_prompt(q, k, v):
    t = q.shape[1]
    outs = []
    for start in range(0, t, Q_BLOCK):
        end = min(start + Q_BLOCK, t)
        outs.append(stick_breaking(q[:, start:end], k[:, :end], v[:, :end],
                                   jnp.arange(start, end), jnp.arange(end)))
    return jnp.concatenate(outs, axis=1)


def sb_sample(q, k, v, k_past, v_past):
    p, t = k_past.shape[1], q.shape[1]
    k_all = jnp.concatenate([k_past, k], axis=1)
    v_all = jnp.concatenate([v_past, v], axis=1)
    return stick_breaking(q, k_all, v_all, p + jnp.arange(t), jnp.arange(p + t))


def pool_mixer(u, history, n_hist, w_pool, pool_scale):
    t = u.shape[1]
    seq = jnp.concatenate([history, u], axis=1)
    csum = jnp.cumsum(seq.astype(jnp.float32), axis=1)
    csum = jnp.pad(csum, ((0, 0), (1, 0), (0, 0)))
    csum_end = csum[:, POOL_STATE + 1:POOL_STATE + 1 + t]
    pos = jnp.arange(t)
    outs = []
    for g, w in enumerate(POOL_WINDOWS):
        sl = slice(g * POOL_GROUP, (g + 1) * POOL_GROUP)
        csum_start = csum[:, POOL_STATE + 1 - w:POOL_STATE + 1 - w + t, sl]
        count = jnp.minimum(n_hist + pos + 1, w).astype(jnp.float32)[None, :, None]
        mean = (csum_end[..., sl] - csum_start) / count
        diff = (mean - u[..., sl].astype(jnp.float32)).astype(u.dtype)
        outs.append(diff @ w_pool[g])
    out = jnp.concatenate(outs, axis=-1) * pool_scale
    return out, seq[:, -POOL_STATE:]


def swiglu(h, w_gate, w_up, w_down):
    return (jax.nn.silu(h @ w_gate) * (h @ w_up)) @ w_down


def trunk(x, c, k_cache, v_cache, pool_cache, w_ada, b_ada, g_mix, g_ffn, w_qkv, g_q, g_k, w_o,
          w_pool, pool_scale, w_gate, w_up, w_down):
    b, t, _ = x.shape
    is_sample = k_cache is not None
    n_hist = k_cache.shape[2] if is_sample else 0
    new_k, new_v, new_pool = [], [], []
    for i in range(DEPTH):
        shift1, scale1, gate1, shift2, scale2, gate2 = ada_modulation(c, w_ada[i], b_ada[i])
        h = rms_norm(x, g_mix[i]) * (1.0 + scale1) + shift1
        j = i // N_MIXERS
        if i % N_MIXERS == 0:
            q, k, v = qkv_project(h, w_qkv[j], g_q[j], g_k[j])
            o = sb_sample(q, k, v, k_cache[j], v_cache[j]) if is_sample else sb_prompt(q, k, v)
            mix = o.reshape(b, t, D_MODEL) @ w_o[j]
            new_k.append(k)
            new_v.append(v)
        else:
            hist = pool_cache[j] if is_sample else jnp.zeros((b, POOL_STATE, D_MODEL), h.dtype)
            mix, st = pool_mixer(h, hist, n_hist, w_pool[j], pool_scale[j])
            new_pool.append(st)
        x = x + gate1 * mix
        h = rms_norm(x, g_ffn[i]) * (1.0 + scale2) + shift2
        x = x + gate2 * swiglu(h, w_gate[i], w_up[i], w_down[i])
    return x, jnp.stack(new_k), jnp.stack(new_v), jnp.stack(new_pool)


def setup_inputs(seed: int = 0) -> dict:
    key = jax.random.key(seed)
    ks = jax.random.split(key, 24)
    f32 = jnp.float32
    nrm = lambda k, shape, s: jax.random.normal(k, shape, f32) * s
    return {
        'x_prompt': nrm(ks[0], (BATCH, SEQ, D_MODEL), 1.0),
        'x_sample': nrm(ks[1], (DEC_BATCH, DEC_SEQ, D_MODEL), 1.0),
        'c_prompt': nrm(ks[2], (BATCH, D_MODEL), 1.0),
        'c_sample': nrm(ks[3], (DEC_BATCH, D_MODEL), 1.0),
        'cache_k': nrm(ks[4], (N_SB_LAYERS, DEC_BATCH, PAST_LEN, N_HEADS, HEAD_DIM), 1.0),
        'cache_v': nrm(ks[5], (N_SB_LAYERS, DEC_BATCH, PAST_LEN, N_HEADS, HEAD_DIM), 1.0),
        'state_pool': nrm(ks[6], (N_POOL_LAYERS, DEC_BATCH, POOL_STATE, D_MODEL), 1.0),
        'w_ada': nrm(ks[7], (DEPTH, D_MODEL, N_MOD * D_MODEL), 0.5 * D_MODEL ** -0.5),
        'b_ada': nrm(ks[8], (DEPTH, N_MOD * D_MODEL), 0.01),
        'g_mix': 1.0 + nrm(ks[9], (DEPTH, D_MODEL), 0.01),
        'g_ffn': 1.0 + nrm(ks[10], (DEPTH, D_MODEL), 0.01),
        'w_qkv': nrm(ks[11], (N_SB_LAYERS, D_MODEL, 3 * D_MODEL), D_MODEL ** -0.5),
        'g_q': 1.0 + nrm(ks[12], (N_SB_LAYERS, HEAD_DIM), 0.01),
        'g_k': 1.0 + nrm(ks[13], (N_SB_LAYERS, HEAD_DIM), 0.01),
        'w_o': nrm(ks[14], (N_SB_LAYERS, D_MODEL, D_MODEL), D_MODEL ** -0.5),
        'w_pool': nrm(ks[15], (N_POOL_LAYERS, N_POOL_GROUPS, POOL_GROUP, POOL_GROUP), POOL_GROUP ** -0.5),
        'pool_scale': 1.0 + nrm(ks[16], (N_POOL_LAYERS, D_MODEL), 0.1),
        'w_gate': nrm(ks[17], (DEPTH, D_MODEL, D_FF), D_MODEL ** -0.5),
        'w_up': nrm(ks[18], (DEPTH, D_MODEL, D_FF), D_MODEL ** -0.5),
        'w_down': nrm(ks[19], (DEPTH, D_FF, D_MODEL), D_FF ** -0.5),
    }


def reference(x_prompt, x_sample, c_prompt, c_sample, cache_k, cache_v, state_pool,
              w_ada, b_ada, g_mix, g_ffn, w_qkv, g_q, g_k, w_o, w_pool, pool_scale,
              w_gate, w_up, w_down):
    y_prompt, k_prompt, v_prompt, pool_prompt = trunk(
        x_prompt, c_prompt, None, None, None, w_ada, b_ada, g_mix, g_ffn, w_qkv, g_q, g_k, w_o,
        w_pool, pool_scale, w_gate, w_up, w_down)
    y_sample, k_sample, v_sample, pool_sample = trunk(
        x_sample, c_sample, cache_k, cache_v, state_pool, w_ada, b_ada, g_mix, g_ffn, w_qkv, g_q, g_k,
        w_o, w_pool, pool_scale, w_gate, w_up, w_down)
    return (y_prompt, y_sample, k_prompt, v_prompt, pool_prompt, k_sample, v_sample, pool_sample)
```

```python
import functools

import jax
import jax.numpy as jnp
from jax import lax
from jax.experimental import pallas as pl
from jax.experimental.pallas import tpu as pltpu

F32 = jnp.float32
BF16 = jnp.bfloat16

HEAD_DIM = 128
EPS = 1e-6
N_MOD = 6
POOL_WINDOWS = (2, 4, 8, 16)
HALO = 16
POOL_STATE = HALO - 1

V7X_VMEM_BYTES = 64 * 1024 * 1024
VMEM_LIMIT = V7X_VMEM_BYTES - 8 * 1024 * 1024

TILE_ROWS = 1024
TILE_COLS = 1024
TILE_QKV_COLS = 512
TILE_FFN_ROWS = 512
TILE_FF = 512
TILE_POOL_ROWS = 512
TILE_Q = 256
TILE_CACHE = 512
TILE_ADA = 1024


def _cparams(sem):
    return pltpu.CompilerParams(dimension_semantics=sem, vmem_limit_bytes=VMEM_LIMIT)


def _normmod(x3, g, shift, scale):
    ms = jnp.mean(x3 * x3, axis=-1, keepdims=True)
    y = x3 * lax.rsqrt(ms + EPS) * g
    return y * (1.0 + scale) + shift


def _later_matrix(n):
    j = jnp.arange(n)[:, None]
    s = jnp.arange(n)[None, :]
    u = (j > s).astype(BF16)
    return jnp.concatenate([u, u], axis=0)


def _sb_weights(z, uu, carry, mask):
    sp = jnp.maximum(z, 0.0) + jnp.log(1.0 + jnp.exp(-jnp.abs(z)))
    spm = sp if mask is None else jnp.where(mask, sp, 0.0)
    hi = spm.astype(BF16)
    lo = (spm - hi.astype(F32)).astype(BF16)
    later = jnp.dot(jnp.concatenate([hi, lo], axis=1), uu, preferred_element_type=F32)
    arg = z - sp - later
    if carry is not None:
        arg = arg + carry
    w = jnp.exp(arg)
    if mask is not None:
        w = jnp.where(mask, w, 0.0)
    return w, jnp.sum(spm, axis=1, keepdims=True)


def _ada_kernel(c_ref, w_ref, b_ref, o_ref):
    c = c_ref[...]
    s = (c * jax.nn.sigmoid(c)).astype(BF16)
    o_ref[...] = jnp.dot(s, w_ref[...].astype(BF16), preferred_element_type=F32) + b_ref[...]


def _ada_modulation(c_all, w_ada, b_ada):
    n_layers, d, _ = w_ada.shape
    bt = c_all.shape[0]
    tn = min(TILE_ADA, d)
    npc = d // tn
    out = pl.pallas_call(
        _ada_kernel,
        grid=(n_layers, N_MOD * npc),
        in_specs=[
            pl.BlockSpec((bt, d), lambda l, n: (0, 0)),
            pl.BlockSpec((None, d, tn), lambda l, n: (l, 0, n)),
            pl.BlockSpec((None, 1, tn), lambda l, n: (l, 0, n)),
        ],
        out_specs=pl.BlockSpec((None, None, bt, tn), lambda l, n: (l, n // npc, 0, n % npc)),
        out_shape=jax.ShapeDtypeStruct((n_layers, N_MOD, bt, d), F32),
        compiler_params=_cparams(("arbitrary", "arbitrary")),
        name="ada_modulation",
    )(c_all, w_ada, b_ada.reshape(n_layers, 1, N_MOD * d))
    return out.reshape(n_layers, N_MOD, bt, 1, d)


def _mod_spec(layer, which, bb, boff, d, ncols=None):
    if ncols is None:
        return pl.BlockSpec((None, None, bb, 1, d), lambda b, i, *_: (layer, which, boff // bb + b, 0, 0))
    return pl.BlockSpec((None, None, bb, 1, ncols), lambda b, i, n: (layer, which, boff // bb + b, 0, n))


def _head_norm(y, g):
    outs = []
    for h in range(y.shape[1] // HEAD_DIM):
        yh = y[:, h * HEAD_DIM:(h + 1) * HEAD_DIM]
        ms = jnp.mean(yh * yh, axis=-1, keepdims=True)
        outs.append(yh * lax.rsqrt(ms + EPS) * g)
    return outs[0] if len(outs) == 1 else jnp.concatenate(outs, axis=1)


def _qkv_kernel(x_ref, sh_ref, sc_ref, g_ref, w_ref, gq_ref, gk_ref, q_ref, k_ref, v_ref, h_scr, *, n_per):
    j = pl.program_id(2)

    @pl.when(j == 0)
    def _():
        h = _normmod(x_ref[...], g_ref[...], sh_ref[...], sc_ref[...])
        h_scr[...] = h.reshape(h_scr.shape).astype(BF16)

    y = jnp.dot(h_scr[...], w_ref[...], preferred_element_type=F32)

    @pl.when(j < n_per)
    def _():
        q_ref[...] = _head_norm(y, gq_ref[...]).astype(BF16).reshape(q_ref.shape)

    @pl.when(jnp.logical_and(j >= n_per, j < 2 * n_per))
    def _():
        k_ref[...] = _head_norm(y, gk_ref[...]).reshape(k_ref.shape)

    @pl.when(j >= 2 * n_per)
    def _():
        v_ref[...] = y.reshape(v_ref.shape)


def _qkv_project(x, mod, boff, bb, tt, g_mix, w_qkv, g_q, g_k):
    b, t, d = x.shape
    tn = min(TILE_QKV_COLS, d)
    n_per = d // tn
    grid = (b // bb, t // tt, 3 * n_per)
    out_specs = [
        pl.BlockSpec((bb, tt, tn), lambda bi, i, j: (bi, i, jnp.minimum(j, n_per - 1))),
        pl.BlockSpec((bb, tt, tn), lambda bi, i, j: (bi, i, jnp.clip(j - n_per, 0, n_per - 1))),
        pl.BlockSpec((bb, tt, tn), lambda bi, i, j: (bi, i, jnp.maximum(j - 2 * n_per, 0))),
    ]
    return pl.pallas_call(
        functools.partial(_qkv_kernel, n_per=n_per),
        grid=grid,
        in_specs=[
            pl.BlockSpec((bb, tt, d), lambda bi, i, j: (bi, i, 0)),
            _mod_spec(0, 0, bb, boff, d),
            _mod_spec(0, 1, bb, boff, d),
            pl.BlockSpec((1, d), lambda bi, i, j: (0, 0)),
            pl.BlockSpec((d, tn), lambda bi, i, j: (0, j)),
            pl.BlockSpec((1, HEAD_DIM), lambda bi, i, j: (0, 0)),
            pl.BlockSpec((1, HEAD_DIM), lambda bi, i, j: (0, 0)),
        ],
        out_specs=out_specs,
        out_shape=[
            jax.ShapeDtypeStruct((b, t, d), BF16),
            jax.ShapeDtypeStruct((b, t, d), F32),
            jax.ShapeDtypeStruct((b, t, d), F32),
        ],
        scratch_shapes=[pltpu.VMEM((bb * tt, d), BF16)],
        compiler_params=_cparams(("parallel", "parallel", "arbitrary")),
        name="qkv_project",
    )(x, mod, mod, g_mix, w_qkv, g_q, g_k)


def _nt_dot(a, b):
    return lax.dot_general(a, b, (((1,), (1,)), ((), ())), preferred_element_type=F32)


def _attn_prompt_kernel(q_ref, k_ref, v_ref, uu_ref, o_ref, kb_scr, vb_scr, *, tq):
    t = q_ref.shape[0]
    scale = HEAD_DIM ** -0.5
    kb_scr[...] = k_ref[...].astype(BF16)
    vb_scr[...] = v_ref[...].astype(BF16)
    uu = uu_ref[...]
    row = lax.broadcasted_iota(jnp.int32, (tq, tq), 0)
    col = lax.broadcasted_iota(jnp.int32, (tq, tq), 1)
    causal = col < row

    for qi in range(t // tq):
        q = q_ref[qi * tq:(qi + 1) * tq, :]
        z = _nt_dot(q, kb_scr[qi * tq:(qi + 1) * tq, :]) * scale
        w, rs = _sb_weights(z, uu, None, causal)
        acc = jnp.dot(w.astype(BF16), vb_scr[qi * tq:(qi + 1) * tq, :], preferred_element_type=F32)
        carry = -rs

        def body(step, state, q=q, qi=qi):
            acc, carry = state
            start = pl.multiple_of((qi - 1 - step) * tq, tq)
            z = _nt_dot(q, kb_scr[pl.ds(start, tq), :]) * scale
            w, rs = _sb_weights(z, uu, carry, None)
            acc = acc + jnp.dot(w.astype(BF16), vb_scr[pl.ds(start, tq), :], preferred_element_type=F32)
            return acc, carry - rs

        if qi > 0:
            acc, carry = lax.fori_loop(0, qi, body, (acc, carry))
        o_ref[qi * tq:(qi + 1) * tq, :] = acc.astype(o_ref.dtype)


def _attn_prompt(q, k, v):
    b, t, d = q.shape
    tq = min(TILE_Q, t)
    spec = pl.BlockSpec((None, t, HEAD_DIM), lambda bi, h: (bi, 0, h))
    return pl.pallas_call(
        functools.partial(_attn_prompt_kernel, tq=tq),
        grid=(b, d // HEAD_DIM),
        in_specs=[spec, spec, spec, pl.BlockSpec((2 * tq, tq), lambda bi, h: (0, 0))],
        out_specs=spec,
        out_shape=jax.ShapeDtypeStruct((b, t, d), BF16),
        scratch_shapes=[pltpu.VMEM((t, HEAD_DIM), BF16), pltpu.VMEM((t, HEAD_DIM), BF16)],
        compiler_params=_cparams(("parallel", "parallel")),
        name="sb_attention_prompt",
    )(q, k, v, _later_matrix(tq))


def _attn_sample_kernel(q_ref, kn_ref, vn_ref, ck_ref, cv_ref, uu_ref, uun_ref, o_ref,
                        qbd_scr, knew_scr, vnew_scr, acc_scr, carry_scr, *, n_heads):
    j = pl.program_id(1)
    tnew, d = q_ref.shape
    rows = n_heads * tnew
    scale = HEAD_DIM ** -0.5

    @pl.when(j == 0)
    def _():
        q = q_ref[...]
        lane_head = lax.broadcasted_iota(jnp.int32, (tnew, d), 1) // HEAD_DIM
        for h in range(n_heads):
            qbd_scr[h * tnew:(h + 1) * tnew, :] = jnp.where(lane_head == h, q, jnp.zeros_like(q))
        knew_scr[...] = jnp.zeros_like(knew_scr)
        vnew_scr[...] = jnp.zeros_like(vnew_scr)
        knew_scr[0:tnew, :] = kn_ref[...].astype(BF16)
        vnew_scr[0:tnew, :] = vn_ref[...].astype(BF16)
        z = _nt_dot(qbd_scr[...], knew_scr[...]) * scale
        t_of_row = lax.broadcasted_iota(jnp.int32, z.shape, 0) % tnew
        s_of_col = lax.broadcasted_iota(jnp.int32, z.shape, 1)
        w, rs = _sb_weights(z, uun_ref[...], None, s_of_col < t_of_row)
        acc_scr[...] = jnp.dot(w.astype(BF16), vnew_scr[...], preferred_element_type=F32)
        carry_scr[...] = -rs

    @pl.when(j > 0)
    def _():
        z = _nt_dot(qbd_scr[...], ck_ref[...].astype(BF16)) * scale
        w, rs = _sb_weights(z, uu_ref[...], carry_scr[...], None)
        acc_scr[...] += jnp.dot(w.astype(BF16), cv_ref[...].astype(BF16), preferred_element_type=F32)
        carry_scr[...] -= rs

    @pl.when(j == pl.num_programs(1) - 1)
    def _():
        for h in range(n_heads):
            cols = slice(h * HEAD_DIM, (h + 1) * HEAD_DIM)
            o_ref[:, cols] = acc_scr[h * tnew:(h + 1) * tnew, cols].astype(o_ref.dtype)


def _attn_sample(q, k_new, v_new, cache_k, cache_v):
    b, t, d = q.shape
    p = cache_k.shape[1]
    n_heads = d // HEAD_DIM
    tk = min(TILE_CACHE, p)
    n_tiles = p // tk
    new_pad = 128
    row_spec = pl.BlockSpec((None, t, d), lambda bi, j: (bi, 0, 0))
    cache_spec = pl.BlockSpec((None, tk, d), lambda bi, j: (bi, n_tiles - jnp.maximum(j, 1), 0))
    return pl.pallas_call(
        functools.partial(_attn_sample_kernel, n_heads=n_heads),
        grid=(b, n_tiles + 1),
        in_specs=[
            row_spec, row_spec, row_spec, cache_spec, cache_spec,
            pl.BlockSpec((2 * tk, tk), lambda bi, j: (0, 0)),
            pl.BlockSpec((2 * new_pad, new_pad), lambda bi, j: (0, 0)),
        ],
        out_specs=row_spec,
        out_shape=jax.ShapeDtypeStruct((b, t, d), BF16),
        scratch_shapes=[
            pltpu.VMEM((n_heads * t, d), BF16),
            pltpu.VMEM((new_pad, d), BF16),
            pltpu.VMEM((new_pad, d), BF16),
            pltpu.VMEM((n_heads * t, d), F32),
            pltpu.VMEM((n_heads * t, 1), F32),
        ],
        compiler_params=_cparams(("parallel", "arbitrary")),
        name="sb_attention_sample",
    )(q, k_new, v_new, cache_k, cache_v, _later_matrix(tk), _later_matrix(new_pad))


def _proj_res_kernel(a_ref, w_ref, x_ref, gate_ref, o_ref):
    bb, tt, k = a_ref.shape
    y = jnp.dot(a_ref[...].reshape(bb * tt, k), w_ref[...], preferred_element_type=F32)
    o_ref[...] = x_ref[...] + gate_ref[...] * y.reshape(o_ref.shape)


def _proj_residual(a, w, x, mod, layer, which, boff, bb, tt):
    b, t, k = a.shape
    d = w.shape[1]
    tn = min(TILE_COLS, d)
    return pl.pallas_call(
        _proj_res_kernel,
        grid=(b // bb, t // tt, d // tn),
        in_specs=[
            pl.BlockSpec((bb, tt, k), lambda bi, i, n: (bi, i, 0)),
            pl.BlockSpec((k, tn), lambda bi, i, n: (0, n)),
            pl.BlockSpec((bb, tt, tn), lambda bi, i, n: (bi, i, n)),
            _mod_spec(layer, which, bb, boff, d, ncols=tn),
        ],
        out_specs=pl.BlockSpec((bb, tt, tn), lambda bi, i, n: (bi, i, n)),
        out_shape=jax.ShapeDtypeStruct((b, t, d), F32),
        compiler_params=_cparams(("parallel", "parallel", "arbitrary")),
        name="proj_residual",
    )(a, w, x, mod)


def _ffn_kernel(x_ref, sh_ref, sc_ref, gt_ref, g_ref, wg_ref, wu_ref, wd_ref, o_ref, h_scr):
    f = pl.program_id(2)

    @pl.when(f == 0)
    def _():
        h = _normmod(x_ref[...], g_ref[...], sh_ref[...], sc_ref[...])
        h_scr[...] = h.reshape(h_scr.shape).astype(BF16)

    h = h_scr[...]
    gate = jnp.dot(h, wg_ref[...], preferred_element_type=F32)
    up = jnp.dot(h, wu_ref[...], preferred_element_type=F32)
    a = (gate * jax.nn.sigmoid(gate) * up).astype(BF16)
    part = jnp.dot(a, wd_ref[...], preferred_element_type=F32).reshape(o_ref.shape)

    @pl.when(f == 0)
    def _():
        o_ref[...] = part

    @pl.when(f > 0)
    def _():
        o_ref[...] += part

    @pl.when(f == pl.num_programs(2) - 1)
    def _():
        o_ref[...] = x_ref[...] + gt_ref[...] * o_ref[...]


def _ffn(x, mod, layer, boff, bb, tt, g_ffn, w_gate, w_up, w_down):
    b, t, d = x.shape
    ff = w_gate.shape[1]
    tf = min(TILE_FF, ff)
    xspec = pl.BlockSpec((bb, tt, d), lambda bi, i, f: (bi, i, 0))
    return pl.pallas_call(
        _ffn_kernel,
        grid=(b // bb, t // tt, ff // tf),
        in_specs=[
            xspec,
            _mod_spec(layer, 3, bb, boff, d),
            _mod_spec(layer, 4, bb, boff, d),
            _mod_spec(layer, 5, bb, boff, d),
            pl.BlockSpec((1, d), lambda bi, i, f: (0, 0)),
            pl.BlockSpec((d, tf), lambda bi, i, f: (0, f)),
            pl.BlockSpec((d, tf), lambda bi, i, f: (0, f)),
            pl.BlockSpec((tf, d), lambda bi, i, f: (f, 0)),
        ],
        out_specs=xspec,
        out_shape=jax.ShapeDtypeStruct((b, t, d), F32),
        scratch_shapes=[pltpu.VMEM((bb * tt, d), BF16)],
        compiler_params=_cparams(("parallel", "parallel", "arbitrary")),
        name="swiglu_ffn",
    )(x, mod, mod, mod, g_ffn, w_gate, w_up, w_down)


def _pool_kernel(x_ref, halo_ref, sh_ref, sc_ref, gt_ref, g_ref, wp_ref, ps_ref, o_ref, tail_ref,
                 *, halo_is_x, n_hist):
    i = pl.program_id(1)
    bb, tt, d = x_ref.shape
    dg = d // len(POOL_WINDOWS)
    seg = HALO + tt
    x = x_ref[...]
    h = _normmod(x, g_ref[...], sh_ref[...], sc_ref[...])
    if halo_is_x:
        hist = _normmod(halo_ref[...], g_ref[...], sh_ref[...], sc_ref[...])
        hist = jnp.where(i > 0, hist, 0.0)
    else:
        hist = halo_ref[...]
    tail_ref[...] = h[:, tt - HALO:, :]

    s_all = jnp.concatenate([hist, h], axis=1).reshape(bb * seg, d)
    h2 = h.reshape(bb * tt, d)
    pos = i * tt + lax.broadcasted_iota(jnp.int32, (bb * tt, 1), 0) % tt
    outs = []
    for g, win in enumerate(POOL_WINDOWS):
        cols = slice(g * dg, (g + 1) * dg)
        s = s_all[:, cols]
        span = 1
        while span < win:
            s = s + pltpu.roll(s, span, 0)
            span *= 2
        s = s.reshape(bb, seg, dg)[:, HALO:, :].reshape(bb * tt, dg)
        count = jnp.minimum(n_hist + pos + 1, win).astype(F32)
        diff = (s / count - h2[:, cols]).astype(BF16)
        outs.append(jnp.dot(diff, wp_ref[g], preferred_element_type=F32))
    mix = (jnp.concatenate(outs, axis=1) * ps_ref[...]).reshape(bb, tt, d)
    o_ref[...] = x + gt_ref[...] * mix


def _pool_layer(x, halo, halo_is_x, n_hist, mod, layer, boff, bb, tt, g_mix, w_pool, pool_scale):
    b, t, d = x.shape
    dg = d // len(POOL_WINDOWS)
    if halo_is_x:
        per = tt // HALO
        halo_spec = pl.BlockSpec((bb, HALO, d), lambda bi, i: (bi, jnp.maximum(i * per - 1, 0), 0))
    else:
        halo_spec = pl.BlockSpec((bb, HALO, d), lambda bi, i: (bi, 0, 0))
    xspec = pl.BlockSpec((bb, tt, d), lambda bi, i: (bi, i, 0))
    return pl.pallas_call(
        functools.partial(_pool_kernel, halo_is_x=halo_is_x, n_hist=n_hist),
        grid=(b // bb, t // tt),
        in_specs=[
            xspec, halo_spec,
            _mod_spec(layer, 0, bb, boff, d),
            _mod_spec(layer, 1, bb, boff, d),
            _mod_spec(layer, 2, bb, boff, d),
            pl.BlockSpec((1, d), lambda bi, i: (0, 0)),
            pl.BlockSpec((len(POOL_WINDOWS), dg, dg), lambda bi, i: (0, 0, 0)),
            pl.BlockSpec((1, d), lambda bi, i: (0, 0)),
        ],
        out_specs=[xspec, pl.BlockSpec((bb, HALO, d), lambda bi, i: (bi, 0, 0))],
        out_shape=[jax.ShapeDtypeStruct((b, t, d), F32), jax.ShapeDtypeStruct((b, HALO, d), F32)],
        compiler_params=_cparams(("parallel", "arbitrary")),
        name="pool_mixer",
    )(x, halo, mod, mod, mod, g_mix, w_pool, pool_scale)


def _trunk(x, mod, boff, bb, tt, ffn_tt, pool_tt, caches, weights):
    (g_mix, g_ffn, w_qkv, g_q, g_k, w_o, w_pool, pool_scale, w_gate, w_up, w_down) = weights
    b, t, d = x.shape
    n_heads = d // HEAD_DIM
    q, k, v = _qkv_project(x, mod, boff, bb, tt, g_mix[0:1], w_qkv[0], g_q, g_k)
    if caches is None:
        o = _attn_prompt(q, k, v)
    else:
        o = _attn_sample(q, k, v, caches[0], caches[1])
    x = _proj_residual(o, w_o[0], x, mod, 0, 2, boff, bb, tt)
    x = _ffn(x, mod, 0, boff, bb, ffn_tt, g_ffn[0:1], w_gate[0], w_up[0], w_down[0])
    if caches is None:
        x, tail = _pool_layer(x, x, True, 0, mod, 1, boff, bb, pool_tt, g_mix[1:2], w_pool[0], pool_scale)
    else:
        hist = jnp.pad(caches[2], ((0, 0), (1, 0), (0, 0)))
        x, tail = _pool_layer(x, hist, False, caches[0].shape[1], mod, 1, boff, bb, pool_tt,
                              g_mix[1:2], w_pool[0], pool_scale)
    x = _ffn(x, mod, 1, boff, bb, ffn_tt, g_ffn[1:2], w_gate[1], w_up[1], w_down[1])
    heads = (1, b, t, n_heads, HEAD_DIM)
    return x, k.reshape(heads), v.reshape(heads), tail[None, :, 1:, :]


def kernel(x_prompt, x_sample, c_prompt, c_sample, cache_k, cache_v, state_pool, w_ada, b_ada, g_mix, g_ffn,
           w_qkv, g_q, g_k, w_o, w_pool, pool_scale, w_gate, w_up, w_down):
    bp, tp, d = x_prompt.shape
    bs, ts, _ = x_sample.shape
    past = cache_k.shape[2]
    assert w_ada.shape[0] == 2 and cache_k.shape[0] == 1 and state_pool.shape[0] == 1
    assert state_pool.shape[2] == POOL_STATE and ts % HALO == 0 and d % (HEAD_DIM * len(POOL_WINDOWS)) == 0

    weights = (g_mix, g_ffn, w_qkv.astype(BF16), g_q, g_k, w_o.astype(BF16), w_pool.astype(BF16), pool_scale,
               w_gate.astype(BF16), w_up.astype(BF16), w_down.astype(BF16))
    mod = _ada_modulation(jnp.concatenate([c_sample, c_prompt], axis=0), w_ada, b_ada)

    y_p, k_p, v_p, pool_p = _trunk(x_prompt, mod, bs, 1, min(TILE_ROWS, tp), min(TILE_FFN_ROWS, tp),
                                   min(TILE_POOL_ROWS, tp), None, weights)
    caches = (cache_k.reshape(bs, past, d), cache_v.reshape(bs, past, d), state_pool[0])
    y_s, k_s, v_s, pool_s = _trunk(x_sample, mod, 0, bs, ts, ts, ts, caches, weights)
    return (y_p, y_s, k_p, v_p, pool_p, k_s, v_s, pool_s)
```

```python
import functools

import jax
import jax.numpy as jnp
from jax import lax
from jax.experimental import pallas as pl
from jax.experimental.pallas import tpu as pltpu

F32 = jnp.float32
BF16 = jnp.bfloat16

HEAD_DIM = 128
EPS = 1e-6
N_MOD = 6
POOL_WINDOWS = (2, 4, 8, 16)
HALO = 16
POOL_STATE = HALO - 1

V7X_VMEM_BYTES = 64 * 1024 * 1024
VMEM_LIMIT = V7X_VMEM_BYTES - 8 * 1024 * 1024

TILE_ROWS = 1024
TILE_COLS = 1024
TILE_QKV_COLS = 512
QKV_SUBTILE = 256
TILE_FFN_ROWS = 512
TILE_FF = 512
FF_SUBTILE = 256
TILE_POOL_ROWS = 512
TILE_Q = 256
HEADS_PER_STEP = 8
TILE_CACHE = 512
TILE_ADA = 1024


def _cparams(sem):
    return pltpu.CompilerParams(dimension_semantics=sem, vmem_limit_bytes=VMEM_LIMIT)


def _normmod(x3, g, shift, scale):
    ms = jnp.mean(x3 * x3, axis=-1, keepdims=True)
    y = x3 * lax.rsqrt(ms + EPS) * g
    return y * (1.0 + scale) + shift


def _later_matrix(n):
    j = jnp.arange(n)[:, None]
    s = jnp.arange(n)[None, :]
    u = (j >= s).astype(BF16)
    return jnp.concatenate([u, u], axis=0)


LOG2_E = 1.4426950408889634
INV_LN2 = LOG2_E
SOFTPLUS2_CLAMP = 64.0
Q_PRESCALE = HEAD_DIM ** -0.5 * LOG2_E


def _sb_blocks(z2s, uu_ref, carries, mask, pv_fn):
    sums, rowsums = [], []
    for z2 in z2s:
        sp = jnp.maximum(z2, jnp.log(1.0 + jnp.exp2(jnp.minimum(z2, SOFTPLUS2_CLAMP))) * INV_LN2)
        if mask is not None:
            sp = jnp.where(mask, sp, 0.0)
        hi = sp.astype(BF16)
        lo = (sp - hi.astype(F32)).astype(BF16)
        sums.append(jnp.dot(jnp.concatenate([hi, lo], axis=1), uu_ref[...], preferred_element_type=F32))
        rowsums.append(jnp.sum(sp, axis=1, keepdims=True))
    outs = []
    for i, (z2, total, rs, carry) in enumerate(zip(z2s, sums, rowsums, carries)):
        arg = z2 - total
        if carry is not None:
            arg = arg + carry
        w = jnp.exp2(arg)
        if mask is not None:
            w = jnp.where(mask, w, 0.0)
        outs.append((pv_fn(i, w.astype(BF16)), rs))
    return outs


def _ada_kernel(c_ref, w_ref, b_ref, o_ref):
    c = c_ref[...]
    s = (c * jax.nn.sigmoid(c)).astype(BF16)
    o_ref[...] = jnp.dot(s, w_ref[...].astype(BF16), preferred_element_type=F32) + b_ref[...]


def _ada_modulation(c_all, w_ada, b_ada):
    n_layers, d, _ = w_ada.shape
    bt = c_all.shape[0]
    tn = min(TILE_ADA, d)
    npc = d // tn
    out = pl.pallas_call(
        _ada_kernel,
        grid=(n_layers, N_MOD * npc),
        in_specs=[
            pl.BlockSpec((bt, d), lambda l, n: (0, 0)),
            pl.BlockSpec((None, d, tn), lambda l, n: (l, 0, n)),
            pl.BlockSpec((None, 1, tn), lambda l, n: (l, 0, n)),
        ],
        out_specs=pl.BlockSpec((None, None, bt, tn), lambda l, n: (l, n // npc, 0, n % npc)),
        out_shape=jax.ShapeDtypeStruct((n_layers, N_MOD, bt, d), F32),
        compiler_params=_cparams(("arbitrary", "arbitrary")),
        name="ada_modulation",
    )(c_all, w_ada, b_ada.reshape(n_layers, 1, N_MOD * d))
    return out.reshape(n_layers, N_MOD, bt, 1, d)


def _mod_spec(layer, which, bb, boff, d, ncols=None):
    if ncols is None:
        return pl.BlockSpec((None, None, bb, 1, d), lambda b, i, *_: (layer, which, boff // bb + b, 0, 0))
    return pl.BlockSpec((None, None, bb, 1, ncols), lambda b, i, n: (layer, which, boff // bb + b, 0, n))


def _head_norm(y, g):
    outs = []
    for h in range(y.shape[1] // HEAD_DIM):
        yh = y[:, h * HEAD_DIM:(h + 1) * HEAD_DIM]
        ms = jnp.mean(yh * yh, axis=-1, keepdims=True)
        outs.append(yh * lax.rsqrt(ms + EPS) * g)
    return outs[0] if len(outs) == 1 else jnp.concatenate(outs, axis=1)


def _qkv_kernel(x_ref, sh_ref, sc_ref, g_ref, w_ref, gq_ref, gk_ref, q_ref, k_ref, v_ref, kb_ref, vb_ref,
                h_scr, *, n_per):
    j = pl.program_id(2)

    @pl.when(j == 0)
    def _():
        h = _normmod(x_ref[...], g_ref[...], sh_ref[...], sc_ref[...])
        h_scr[...] = h.reshape(h_scr.shape).astype(BF16)

    bb, tt, tn = q_ref.shape
    sub = min(QKV_SUBTILE, tn)

    def project(c):
        return jnp.dot(h_scr[...], w_ref[:, c * sub:(c + 1) * sub], preferred_element_type=F32)

    def pipelined(epilogue):
        y = project(0)
        for c in range(tn // sub):
            nxt = project(c + 1) if (c + 1) * sub < tn else None
            epilogue(y, slice(c * sub, (c + 1) * sub))
            y = nxt

    def store_q(y, cols):
        q_ref[:, :, cols] = _head_norm(y, gq_ref[...] * Q_PRESCALE).astype(BF16).reshape(bb, tt, sub)

    def store_k(y, cols):
        k = _head_norm(y, gk_ref[...])
        k_ref[:, :, cols] = k.reshape(bb, tt, sub)
        kb_ref[:, :, cols] = k.astype(BF16).reshape(bb, tt, sub)

    def store_v(y, cols):
        v_ref[:, :, cols] = y.reshape(bb, tt, sub)
        vb_ref[:, :, cols] = y.astype(BF16).reshape(bb, tt, sub)

    pl.when(j < n_per)(lambda: pipelined(store_q))
    pl.when(jnp.logical_and(j >= n_per, j < 2 * n_per))(lambda: pipelined(store_k))
    pl.when(j >= 2 * n_per)(lambda: pipelined(store_v))


def _qkv_project(x, mod, boff, bb, tt, g_mix, w_qkv, g_q, g_k):
    b, t, d = x.shape
    tn = min(TILE_QKV_COLS, d)
    n_per = d // tn
    grid = (b // bb, t // tt, 3 * n_per)
    q_spec = pl.BlockSpec((bb, tt, tn), lambda bi, i, j: (bi, i, jnp.minimum(j, n_per - 1)))
    k_spec = pl.BlockSpec((bb, tt, tn), lambda bi, i, j: (bi, i, jnp.clip(j - n_per, 0, n_per - 1)))
    v_spec = pl.BlockSpec((bb, tt, tn), lambda bi, i, j: (bi, i, jnp.maximum(j - 2 * n_per, 0)))
    out_specs = [q_spec, k_spec, v_spec, k_spec, v_spec]
    return pl.pallas_call(
        functools.partial(_qkv_kernel, n_per=n_per),
        grid=grid,
        in_specs=[
            pl.BlockSpec((bb, tt, d), lambda bi, i, j: (bi, i, 0)),
            _mod_spec(0, 0, bb, boff, d),
            _mod_spec(0, 1, bb, boff, d),
            pl.BlockSpec((1, d), lambda bi, i, j: (0, 0)),
            pl.BlockSpec((d, tn), lambda bi, i, j: (0, j)),
            pl.BlockSpec((1, HEAD_DIM), lambda bi, i, j: (0, 0)),
            pl.BlockSpec((1, HEAD_DIM), lambda bi, i, j: (0, 0)),
        ],
        out_specs=out_specs,
        out_shape=[
            jax.ShapeDtypeStruct((b, t, d), BF16),
            jax.ShapeDtypeStruct((b, t, d), F32),
            jax.ShapeDtypeStruct((b, t, d), F32),
            jax.ShapeDtypeStruct((b, t, d), BF16),
            jax.ShapeDtypeStruct((b, t, d), BF16),
        ],
        scratch_shapes=[pltpu.VMEM((bb * tt, d), BF16)],
        compiler_params=_cparams(("parallel", "parallel", "arbitrary")),
        name="qkv_project",
    )(x, mod, mod, g_mix, w_qkv, g_q, g_k)


def _nt_dot(a, b):
    return lax.dot_general(a, b, (((1,), (1,)), ((), ())), preferred_element_type=F32)


def _attn_prompt_kernel(q_ref, k_ref, v_ref, uu_ref, o_ref, acc_scr, carry_scr, *, tq, n_par):
    t = q_ref.shape[0]
    row = lax.broadcasted_iota(jnp.int32, (tq, tq), 0)
    col = lax.broadcasted_iota(jnp.int32, (tq, tq), 1)
    causal = col < row
    heads = [slice(h * HEAD_DIM, (h + 1) * HEAD_DIM) for h in range(n_par)]

    def blocks(qrows, krows, first):
        z2s = [_nt_dot(q_ref[qrows, hc], k_ref[krows, hc]) for hc in heads]
        carries = [None if first else carry_scr[i] for i in range(n_par)]
        res = _sb_blocks(z2s, uu_ref, carries, causal if first else None,
                         lambda i, w: jnp.dot(w, v_ref[krows, heads[i]], preferred_element_type=F32))
        for i, (pv, rs) in enumerate(res):
            if first:
                acc_scr[i] = pv
                carry_scr[i] = -rs
            else:
                acc_scr[i] += pv
                carry_scr[i] -= rs

    def query_block(qi, _):
        qrows = pl.ds(pl.multiple_of(qi * tq, tq), tq)
        blocks(qrows, qrows, True)

        def body(step, _):
            blocks(qrows, pl.ds(pl.multiple_of((qi - 1 - step) * tq, tq), tq), False)
            return 0

        lax.fori_loop(0, qi, body, 0)
        for i, hc in enumerate(heads):
            o_ref[qrows, hc] = acc_scr[i].astype(o_ref.dtype)
        return 0

    lax.fori_loop(0, t // tq, query_block, 0)


def _attn_prompt(q, k, v):
    b, t, d = q.shape
    tq = min(TILE_Q, t)
    n_par = HEADS_PER_STEP
    width = n_par * HEAD_DIM
    spec = pl.BlockSpec((None, t, width), lambda bi, h: (bi, 0, h))
    return pl.pallas_call(
        functools.partial(_attn_prompt_kernel, tq=tq, n_par=n_par),
        grid=(b, d // width),
        in_specs=[spec, spec, spec, pl.BlockSpec((2 * tq, tq), lambda bi, h: (0, 0))],
        out_specs=spec,
        out_shape=jax.ShapeDtypeStruct((b, t, d), BF16),
        scratch_shapes=[pltpu.VMEM((n_par, tq, HEAD_DIM), F32), pltpu.VMEM((n_par, tq, 1), F32)],
        compiler_params=_cparams(("parallel", "parallel")),
        name="sb_attention_prompt",
    )(q, k, v, _later_matrix(tq))


def _attn_sample_kernel(q_ref, kn_ref, vn_ref, ck_ref, cv_ref, uu_ref, uun_ref, o_ref,
                        qbd_scr, knew_scr, vnew_scr, acc_scr, carry_scr, *, n_heads):
    j = pl.program_id(1)
    tnew, d = q_ref.shape
    tk = ck_ref.shape[0] // n_heads

    def all_heads(ref):
        return jnp.concatenate([ref[pl.ds(h, tk, stride=n_heads), :] for h in range(n_heads)],
                               axis=1).astype(BF16)

    @pl.when(j == 0)
    def _():
        q = q_ref[...]
        lane_head = lax.broadcasted_iota(jnp.int32, (tnew, d), 1) // HEAD_DIM
        for h in range(n_heads):
            qbd_scr[h * tnew:(h + 1) * tnew, :] = jnp.where(lane_head == h, q, jnp.zeros_like(q))
        knew_scr[...] = jnp.zeros_like(knew_scr)
        vnew_scr[...] = jnp.zeros_like(vnew_scr)
        knew_scr[0:tnew, :] = kn_ref[...].astype(BF16)
        vnew_scr[0:tnew, :] = vn_ref[...].astype(BF16)
        z2 = _nt_dot(qbd_scr[...], knew_scr[...])
        t_of_row = lax.broadcasted_iota(jnp.int32, z2.shape, 0) % tnew
        s_of_col = lax.broadcasted_iota(jnp.int32, z2.shape, 1)
        [(pv, rs)] = _sb_blocks([z2], uun_ref, [None], s_of_col < t_of_row,
                                lambda i, w: jnp.dot(w, vnew_scr[...], preferred_element_type=F32))
        acc_scr[...] = pv
        carry_scr[...] = -rs

    @pl.when(j > 0)
    def _():
        z2 = _nt_dot(qbd_scr[...], all_heads(ck_ref))
        [(pv, rs)] = _sb_blocks([z2], uu_ref, [carry_scr[...]], None,
                                lambda i, w: jnp.dot(w, all_heads(cv_ref), preferred_element_type=F32))
        acc_scr[...] += pv
        carry_scr[...] -= rs

    @pl.when(j == pl.num_programs(1) - 1)
    def _():
        for h in range(n_heads):
            cols = slice(h * HEAD_DIM, (h + 1) * HEAD_DIM)
            o_ref[:, cols] = acc_scr[h * tnew:(h + 1) * tnew, cols].astype(o_ref.dtype)


def _attn_sample(q, k_new, v_new, cache_k, cache_v):
    b, t, d = q.shape
    n_heads = d // HEAD_DIM
    p = cache_k.shape[1] // n_heads
    tk = min(TILE_CACHE, p)
    n_tiles = p // tk
    new_pad = 128
    row_spec = pl.BlockSpec((None, t, d), lambda bi, j: (bi, 0, 0))
    cache_spec = pl.BlockSpec((None, tk * n_heads, HEAD_DIM),
                              lambda bi, j: (bi, n_tiles - jnp.maximum(j, 1), 0))
    return pl.pallas_call(
        functools.partial(_attn_sample_kernel, n_heads=n_heads),
        grid=(b, n_tiles + 1),
        in_specs=[
            row_spec, row_spec, row_spec, cache_spec, cache_spec,
            pl.BlockSpec((2 * tk, tk), lambda bi, j: (0, 0)),
            pl.BlockSpec((2 * new_pad, new_pad), lambda bi, j: (0, 0)),
        ],
        out_specs=row_spec,
        out_shape=jax.ShapeDtypeStruct((b, t, d), BF16),
        scratch_shapes=[
            pltpu.VMEM((n_heads * t, d), BF16),
            pltpu.VMEM((new_pad, d), BF16),
            pltpu.VMEM((new_pad, d), BF16),
            pltpu.VMEM((n_heads * t, d), F32),
            pltpu.VMEM((n_heads * t, 1), F32),
        ],
        compiler_params=_cparams(("parallel", "arbitrary")),
        name="sb_attention_sample",
    )(q, k_new, v_new, cache_k, cache_v, _later_matrix(tk), _later_matrix(new_pad))


def _proj_res_kernel(a_ref, w_ref, x_ref, gate_ref, o_ref):
    bb, tt, k = a_ref.shape
    y = jnp.dot(a_ref[...].reshape(bb * tt, k), w_ref[...], preferred_element_type=F32)
    o_ref[...] = x_ref[...] + gate_ref[...] * y.reshape(o_ref.shape)


def _proj_residual(a, w, x, mod, layer, which, boff, bb, tt):
    b, t, k = a.shape
    d = w.shape[1]
    tn = min(TILE_COLS, d)
    return pl.pallas_call(
        _proj_res_kernel,
        grid=(b // bb, t // tt, d // tn),
        in_specs=[
            pl.BlockSpec((bb, tt, k), lambda bi, i, n: (bi, i, 0)),
            pl.BlockSpec((k, tn), lambda bi, i, n: (0, n)),
            pl.BlockSpec((bb, tt, tn), lambda bi, i, n: (bi, i, n)),
            _mod_spec(layer, which, bb, boff, d, ncols=tn),
        ],
        out_specs=pl.BlockSpec((bb, tt, tn), lambda bi, i, n: (bi, i, n)),
        out_shape=jax.ShapeDtypeStruct((b, t, d), F32),
        compiler_params=_cparams(("parallel", "parallel", "arbitrary")),
        name="proj_residual",
    )(a, w, x, mod)


def _ffn_kernel(x_ref, sh_ref, sc_ref, gt_ref, g_ref, wg_ref, wu_ref, wd_ref, o_ref, h_scr):
    f = pl.program_id(2)

    @pl.when(f == 0)
    def _():
        h = _normmod(x_ref[...], g_ref[...], sh_ref[...], sc_ref[...])
        h_scr[...] = h.reshape(h_scr.shape).astype(BF16)
        o_ref[...] = jnp.zeros_like(o_ref)

    tf = wg_ref.shape[1]
    sub = min(FF_SUBTILE, tf)

    def gate_up(c):
        cols = slice(c * sub, (c + 1) * sub)
        h = h_scr[...]
        return (jnp.dot(h, wg_ref[:, cols], preferred_element_type=F32),
                jnp.dot(h, wu_ref[:, cols], preferred_element_type=F32))

    gate, up = gate_up(0)
    for c in range(tf // sub):
        nxt = gate_up(c + 1) if (c + 1) * sub < tf else None
        a = (gate * jax.nn.sigmoid(gate) * up).astype(BF16)
        o_ref[...] += jnp.dot(a, wd_ref[c * sub:(c + 1) * sub, :], preferred_element_type=F32).reshape(o_ref.shape)
        if nxt is not None:
            gate, up = nxt

    @pl.when(f == pl.num_programs(2) - 1)
    def _():
        o_ref[...] = x_ref[...] + gt_ref[...] * o_ref[...]


def _ffn(x, mod, layer, boff, bb, tt, g_ffn, w_gate, w_up, w_down):
    b, t, d = x.shape
    ff = w_gate.shape[1]
    tf = min(TILE_FF, ff)
    xspec = pl.BlockSpec((bb, tt, d), lambda bi, i, f: (bi, i, 0))
    return pl.pallas_call(
        _ffn_kernel,
        grid=(b // bb, t // tt, ff // tf),
        in_specs=[
            xspec,
            _mod_spec(layer, 3, bb, boff, d),
            _mod_spec(layer, 4, bb, boff, d),
            _mod_spec(layer, 5, bb, boff, d),
            pl.BlockSpec((1, d), lambda bi, i, f: (0, 0)),
            pl.BlockSpec((d, tf), lambda bi, i, f: (0, f)),
            pl.BlockSpec((d, tf), lambda bi, i, f: (0, f)),
            pl.BlockSpec((tf, d), lambda bi, i, f: (f, 0)),
        ],
        out_specs=xspec,
        out_shape=jax.ShapeDtypeStruct((b, t, d), F32),
        scratch_shapes=[pltpu.VMEM((bb * tt, d), BF16)],
        compiler_params=_cparams(("parallel", "parallel", "arbitrary")),
        name="swiglu_ffn",
    )(x, mod, mod, mod, g_ffn, w_gate, w_up, w_down)


def _pool_kernel(x_ref, halo_ref, sh_ref, sc_ref, gt_ref, g_ref, wp_ref, ps_ref, o_ref, tail_ref,
                 *, halo_is_x, n_hist):
    i = pl.program_id(1)
    bb, tt, d = x_ref.shape
    dg = d // len(POOL_WINDOWS)
    seg = HALO + tt
    x = x_ref[...]
    h = _normmod(x, g_ref[...], sh_ref[...], sc_ref[...])
    if halo_is_x:
        hist = _normmod(halo_ref[...], g_ref[...], sh_ref[...], sc_ref[...])
        hist = jnp.where(i > 0, hist, 0.0)
    else:
        hist = halo_ref[...]
    tail_ref[...] = h[:, tt - HALO:, :]

    s_all = jnp.concatenate([hist, h], axis=1).reshape(bb * seg, d)
    h2 = h.reshape(bb * tt, d)
    pos = i * tt + lax.broadcasted_iota(jnp.int32, (bb * tt, 1), 0) % tt
    outs = []
    for g, win in enumerate(POOL_WINDOWS):
        cols = slice(g * dg, (g + 1) * dg)
        s = s_all[:, cols]
        span = 1
        while span < win:
            s = s + pltpu.roll(s, span, 0)
            span *= 2
        s = s.reshape(bb, seg, dg)[:, HALO:, :].reshape(bb * tt, dg)
        count = jnp.minimum(n_hist + pos + 1, win).astype(F32)
        diff = (s / count - h2[:, cols]).astype(BF16)
        outs.append(jnp.dot(diff, wp_ref[g], preferred_element_type=F32))
    mix = (jnp.concatenate(outs, axis=1) * ps_ref[...]).reshape(bb, tt, d)
    o_ref[...] = x + gt_ref[...] * mix


def _pool_layer(x, halo, halo_is_x, n_hist, mod, layer, boff, bb, tt, g_mix, w_pool, pool_scale):
    b, t, d = x.shape
    dg = d // len(POOL_WINDOWS)
    if halo_is_x:
        per = tt // HALO
        halo_spec = pl.BlockSpec((bb, HALO, d), lambda bi, i: (bi, jnp.maximum(i * per - 1, 0), 0))
    else:
        halo_spec = pl.BlockSpec((bb, HALO, d), lambda bi, i: (bi, 0, 0))
    xspec = pl.BlockSpec((bb, tt, d), lambda bi, i: (bi, i, 0))
    return pl.pallas_call(
        functools.partial(_pool_kernel, halo_is_x=halo_is_x, n_hist=n_hist),
        grid=(b // bb, t // tt),
        in_specs=[
            xspec, halo_spec,
            _mod_spec(layer, 0, bb, boff, d),
            _mod_spec(layer, 1, bb, boff, d),
            _mod_spec(layer, 2, bb, boff, d),
            pl.BlockSpec((1, d), lambda bi, i: (0, 0)),
            pl.BlockSpec((len(POOL_WINDOWS), dg, dg), lambda bi, i: (0, 0, 0)),
            pl.BlockSpec((1, d), lambda bi, i: (0, 0)),
        ],
        out_specs=[xspec, pl.BlockSpec((bb, HALO, d), lambda bi, i: (bi, 0, 0))],
        out_shape=[jax.ShapeDtypeStruct((b, t, d), F32), jax.ShapeDtypeStruct((b, HALO, d), F32)],
        compiler_params=_cparams(("parallel", "arbitrary")),
        name="pool_mixer",
    )(x, halo, mod, mod, mod, g_mix, w_pool, pool_scale)


def _trunk(x, mod, boff, bb, tt, ffn_tt, pool_tt, caches, weights):
    (g_mix, g_ffn, w_qkv, g_q, g_k, w_o, w_pool, pool_scale, w_gate, w_up, w_down) = weights
    b, t, d = x.shape
    n_heads = d // HEAD_DIM
    q, k, v, k_bf, v_bf = _qkv_project(x, mod, boff, bb, tt, g_mix[0:1], w_qkv[0], g_q, g_k)
    if caches is None:
        o = _attn_prompt(q, k_bf, v_bf)
    else:
        o = _attn_sample(q, k, v, caches[0], caches[1])
    x = _proj_residual(o, w_o[0], x, mod, 0, 2, boff, bb, tt)
    x = _ffn(x, mod, 0, boff, bb, ffn_tt, g_ffn[0:1], w_gate[0], w_up[0], w_down[0])
    if caches is None:
        x, tail = _pool_layer(x, x, True, 0, mod, 1, boff, bb, pool_tt, g_mix[1:2], w_pool[0], pool_scale)
    else:
        hist = jnp.pad(caches[2], ((0, 0), (1, 0), (0, 0)))
        x, tail = _pool_layer(x, hist, False, caches[0].shape[1] // n_heads, mod, 1, boff, bb, pool_tt,
                              g_mix[1:2], w_pool[0], pool_scale)
    x = _ffn(x, mod, 1, boff, bb, ffn_tt, g_ffn[1:2], w_gate[1], w_up[1], w_down[1])
    heads = (1, b, t, n_heads, HEAD_DIM)
    return x, k.reshape(heads), v.reshape(heads), tail[None, :, 1:, :]


def kernel(x_prompt, x_sample, c_prompt, c_sample, cache_k, cache_v, state_pool, w_ada, b_ada, g_mix, g_ffn,
           w_qkv, g_q, g_k, w_o, w_pool, pool_scale, w_gate, w_up, w_down):
    bp, tp, d = x_prompt.shape
    bs, ts, _ = x_sample.shape
    past = cache_k.shape[2]
    assert w_ada.shape[0] == 2 and cache_k.shape[0] == 1 and state_pool.shape[0] == 1
    assert state_pool.shape[2] == POOL_STATE and ts % HALO == 0 and d % (HEAD_DIM * len(POOL_WINDOWS)) == 0

    weights = (g_mix, g_ffn, w_qkv.astype(BF16), g_q, g_k, w_o.astype(BF16), w_pool.astype(BF16), pool_scale,
               w_gate.astype(BF16), w_up.astype(BF16), w_down.astype(BF16))
    mod = _ada_modulation(jnp.concatenate([c_sample, c_prompt], axis=0), w_ada, b_ada)

    y_p, k_p, v_p, pool_p = _trunk(x_prompt, mod, bs, 1, min(TILE_ROWS, tp), min(TILE_FFN_ROWS, tp),
                                   min(TILE_POOL_ROWS, tp), None, weights)
    n_heads = d // HEAD_DIM
    caches = (cache_k.reshape(bs, past * n_heads, HEAD_DIM), cache_v.reshape(bs, past * n_heads, HEAD_DIM),
              state_pool[0])
    y_s, k_s, v_s, pool_s = _trunk(x_sample, mod, 0, bs, ts, ts, ts, caches, weights)
    return (y_p, y_s, k_p, v_p, pool_p, k_s, v_s, pool_s)
```

```python
import functools
from typing import NamedTuple

import jax
import jax.numpy as jnp
from jax import lax
from jax.experimental import pallas as pl
from jax.experimental.pallas import tpu as pltpu

F32 = jnp.float32
BF16 = jnp.bfloat16

HEAD_DIM = 128
SUBLANES = 8
EPS = 1e-6
N_MOD = 6
POOL_WINDOWS = (2, 4, 8, 16)
HALO = 16
POOL_STATE = HALO - 1

V7X_VMEM_BYTES = 64 * 1024 * 1024
VMEM_LIMIT = V7X_VMEM_BYTES - 8 * 1024 * 1024

TILE_ROWS = 1024
TILE_COLS = 1024
TILE_QKV_ROWS = 512
TILE_QKV_COLS = 1024
TILE_QKV_COLS_F32 = 512
QKV_SUBTILE = 256
FF_DOWN_COLS = 512
TILE_FFN_ROWS = 512
TILE_FF = 512
TILE_FF_F32 = 256
FF_SUBTILE = 256
TILE_POOL_ROWS = 512
TILE_Q = 256
HEADS_PER_STEP = 8
TILE_CACHE = 512
TILE_ADA = 1024


def _cparams(sem):
    return pltpu.CompilerParams(dimension_semantics=sem, vmem_limit_bytes=VMEM_LIMIT)


def _normmod(x3, g, shift, scale):
    ms = jnp.mean(x3 * x3, axis=-1, keepdims=True)
    y = x3 * lax.rsqrt(ms + EPS) * g
    return y * (1.0 + scale) + shift


def _later_matrix(n):
    j = jnp.arange(n)[:, None]
    s = jnp.arange(n)[None, :]
    u = (j >= s).astype(BF16)
    return jnp.concatenate([u, u], axis=0)


LOG2_E = 1.4426950408889634
INV_LN2 = LOG2_E
SOFTPLUS2_CLAMP = 64.0
Q_PRESCALE = HEAD_DIM ** -0.5 * LOG2_E


def _sb_blocks(z2s, uu_ref, carries, mask, pv_fn):
    sums, rowsums = [], []
    for z2 in z2s:
        sp = jnp.maximum(z2, jnp.log(1.0 + jnp.exp2(jnp.minimum(z2, SOFTPLUS2_CLAMP))) * INV_LN2)
        if mask is not None:
            sp = jnp.where(mask, sp, 0.0)
        hi = sp.astype(BF16)
        lo = (sp - hi.astype(F32)).astype(BF16)
        sums.append(jnp.dot(jnp.concatenate([hi, lo], axis=1), uu_ref[...], preferred_element_type=F32))
        rowsums.append(jnp.sum(sp, axis=1, keepdims=True))
    outs = []
    for i, (z2, total, rs, carry) in enumerate(zip(z2s, sums, rowsums, carries)):
        arg = z2 - total
        if carry is not None:
            arg = arg + carry
        w = jnp.exp2(arg)
        if mask is not None:
            w = jnp.where(mask, w, 0.0)
        outs.append((pv_fn(i, w.astype(BF16)), rs))
    return outs


def _ada_kernel(c_ref, w_ref, b_ref, o_ref):
    c = c_ref[...]
    s = (c * jax.nn.sigmoid(c)).astype(BF16)
    o_ref[...] = jnp.dot(s, w_ref[...].astype(BF16), preferred_element_type=F32) + b_ref[...]


def _ada_modulation(c_all, w_ada, b_ada):
    n_layers, d, _ = w_ada.shape
    bt = c_all.shape[0]
    tn = min(TILE_ADA, d)
    npc = d // tn
    out = pl.pallas_call(
        _ada_kernel,
        grid=(n_layers, N_MOD * npc),
        in_specs=[
            pl.BlockSpec((bt, d), lambda l, n: (0, 0)),
            pl.BlockSpec((None, d, tn), lambda l, n: (l, 0, n)),
            pl.BlockSpec((None, 1, tn), lambda l, n: (l, 0, n)),
        ],
        out_specs=pl.BlockSpec((None, None, bt, tn), lambda l, n: (l, n // npc, 0, n % npc)),
        out_shape=jax.ShapeDtypeStruct((n_layers, N_MOD, bt, d), F32),
        compiler_params=_cparams(("arbitrary", "arbitrary")),
        name="ada_modulation",
    )(c_all, w_ada, b_ada.reshape(n_layers, 1, N_MOD * d))
    return out.reshape(n_layers, N_MOD, bt, 1, d)


def _mod_spec(layer, which, bb, boff, d, ncols=None):
    if ncols is None:
        return pl.BlockSpec((None, None, bb, 1, d), lambda b, i, *_: (layer, which, boff // bb + b, 0, 0))
    return pl.BlockSpec((None, None, bb, 1, ncols), lambda b, i, n: (layer, which, boff // bb + b, 0, n))


def _head_norm(y, g):
    outs = []
    for h in range(y.shape[1] // HEAD_DIM):
        yh = y[:, h * HEAD_DIM:(h + 1) * HEAD_DIM]
        ms = jnp.mean(yh * yh, axis=-1, keepdims=True)
        outs.append(yh * lax.rsqrt(ms + EPS) * g)
    return outs[0] if len(outs) == 1 else jnp.concatenate(outs, axis=1)


def _bf16_weight(w_ref, wb_ref):
    if wb_ref is None:
        return w_ref
    wb_ref[...] = w_ref[...].astype(BF16)
    return wb_ref


def _qkv_kernel(x_ref, sh_ref, sc_ref, g_ref, w_ref, gq_ref, gk_ref, q_ref, k_ref, v_ref, kb_ref, vb_ref,
                *rest, n_per):
    wb_ref, h_scr = rest if len(rest) == 2 else (None, *rest)
    j = pl.program_id(2)

    @pl.when(j == 0)
    def _():
        h = _normmod(x_ref[...], g_ref[...], sh_ref[...], sc_ref[...])
        h_scr[...] = h.reshape(h_scr.shape).astype(BF16)

    bb, tt, tn = q_ref.shape
    sub = min(QKV_SUBTILE, tn)
    w_bf = _bf16_weight(w_ref, wb_ref)

    def project(c):
        return jnp.dot(h_scr[...], w_bf[:, c * sub:(c + 1) * sub], preferred_element_type=F32)

    def pipelined(epilogue):
        y = project(0)
        for c in range(tn // sub):
            nxt = project(c + 1) if (c + 1) * sub < tn else None
            epilogue(y, slice(c * sub, (c + 1) * sub))
            y = nxt

    def store_q(y, cols):
        q_ref[:, :, cols] = _head_norm(y, gq_ref[...] * Q_PRESCALE).astype(BF16).reshape(bb, tt, sub)

    def store_k(y, cols):
        k = _head_norm(y, gk_ref[...])
        k_ref[:, :, cols] = k.reshape(bb, tt, sub)
        kb_ref[:, :, cols] = k.astype(BF16).reshape(bb, tt, sub)

    def store_v(y, cols):
        v_ref[:, :, cols] = y.reshape(bb, tt, sub)
        vb_ref[:, :, cols] = y.astype(BF16).reshape(bb, tt, sub)

    pl.when(j < n_per)(lambda: pipelined(store_q))
    pl.when(jnp.logical_and(j >= n_per, j < 2 * n_per))(lambda: pipelined(store_k))
    pl.when(j >= 2 * n_per)(lambda: pipelined(store_v))


def _qkv_project(x, mod, boff, bb, tt, tn, g_mix, w_qkv, g_q, g_k):
    b, t, d = x.shape
    n_per = d // tn
    emit_w = w_qkv.dtype != BF16
    grid = (b // bb, t // tt, 3 * n_per)
    q_spec = pl.BlockSpec((bb, tt, tn), lambda bi, i, j: (bi, i, jnp.minimum(j, n_per - 1)))
    k_spec = pl.BlockSpec((bb, tt, tn), lambda bi, i, j: (bi, i, jnp.clip(j - n_per, 0, n_per - 1)))
    v_spec = pl.BlockSpec((bb, tt, tn), lambda bi, i, j: (bi, i, jnp.maximum(j - 2 * n_per, 0)))
    w_spec = pl.BlockSpec((d, tn), lambda bi, i, j: (0, j))
    out_specs = [q_spec, k_spec, v_spec, k_spec, v_spec]
    out_shape = [
        jax.ShapeDtypeStruct((b, t, d), BF16),
        jax.ShapeDtypeStruct((b, t, d), F32),
        jax.ShapeDtypeStruct((b, t, d), F32),
        jax.ShapeDtypeStruct((b, t, d), BF16),
        jax.ShapeDtypeStruct((b, t, d), BF16),
    ]
    if emit_w:
        assert grid[:2] == (1, 1), "each weight tile must be visited exactly once to emit its bf16 copy"
        out_specs.append(w_spec)
        out_shape.append(jax.ShapeDtypeStruct(w_qkv.shape, BF16))
    return pl.pallas_call(
        functools.partial(_qkv_kernel, n_per=n_per),
        grid=grid,
        in_specs=[
            pl.BlockSpec((bb, tt, d), lambda bi, i, j: (bi, i, 0)),
            _mod_spec(0, 0, bb, boff, d),
            _mod_spec(0, 1, bb, boff, d),
            pl.BlockSpec((1, d), lambda bi, i, j: (0, 0)),
            w_spec,
            pl.BlockSpec((1, HEAD_DIM), lambda bi, i, j: (0, 0)),
            pl.BlockSpec((1, HEAD_DIM), lambda bi, i, j: (0, 0)),
        ],
        out_specs=out_specs,
        out_shape=out_shape,
        scratch_shapes=[pltpu.VMEM((bb * tt, d), BF16)],
        compiler_params=_cparams(("parallel", "parallel", "arbitrary")),
        name="qkv_project",
    )(x, mod, mod, g_mix, w_qkv, g_q, g_k)


def _nt_dot(a, b):
    return lax.dot_general(a, b, (((1,), (1,)), ((), ())), preferred_element_type=F32)


def _attn_prompt_kernel(q_ref, k_ref, v_ref, uu_ref, o_ref, acc_scr, carry_scr, *, tq, n_par):
    t = q_ref.shape[0]
    row = lax.broadcasted_iota(jnp.int32, (tq, tq), 0)
    col = lax.broadcasted_iota(jnp.int32, (tq, tq), 1)
    causal = col < row
    heads = [slice(h * HEAD_DIM, (h + 1) * HEAD_DIM) for h in range(n_par)]

    def blocks(qrows, krows, first):
        z2s = [_nt_dot(q_ref[qrows, hc], k_ref[krows, hc]) for hc in heads]
        carries = [None if first else carry_scr[i] for i in range(n_par)]
        res = _sb_blocks(z2s, uu_ref, carries, causal if first else None,
                         lambda i, w: jnp.dot(w, v_ref[krows, heads[i]], preferred_element_type=F32))
        for i, (pv, rs) in enumerate(res):
            if first:
                acc_scr[i] = pv
                carry_scr[i] = -rs
            else:
                acc_scr[i] += pv
                carry_scr[i] -= rs

    def query_block(qi, _):
        qrows = pl.ds(pl.multiple_of(qi * tq, tq), tq)
        blocks(qrows, qrows, True)

        def body(step, _):
            blocks(qrows, pl.ds(pl.multiple_of((qi - 1 - step) * tq, tq), tq), False)
            return 0

        lax.fori_loop(0, qi, body, 0)
        for i, hc in enumerate(heads):
            o_ref[qrows, hc] = acc_scr[i].astype(o_ref.dtype)
        return 0

    lax.fori_loop(0, t // tq, query_block, 0)


def _attn_prompt(q, k, v):
    b, t, d = q.shape
    tq = min(TILE_Q, t)
    n_par = HEADS_PER_STEP
    width = n_par * HEAD_DIM
    spec = pl.BlockSpec((None, t, width), lambda bi, h: (bi, 0, h))
    return pl.pallas_call(
        functools.partial(_attn_prompt_kernel, tq=tq, n_par=n_par),
        grid=(b, d // width),
        in_specs=[spec, spec, spec, pl.BlockSpec((2 * tq, tq), lambda bi, h: (0, 0))],
        out_specs=spec,
        out_shape=jax.ShapeDtypeStruct((b, t, d), BF16),
        scratch_shapes=[pltpu.VMEM((n_par, tq, HEAD_DIM), F32), pltpu.VMEM((n_par, tq, 1), F32)],
        compiler_params=_cparams(("parallel", "parallel")),
        name="sb_attention_prompt",
    )(q, k, v, _later_matrix(tq))


def _head_rows(group_ref, i):
    tk = group_ref.shape[0]
    return group_ref.reshape(tk * SUBLANES, HEAD_DIM)[pl.ds(i, tk, stride=SUBLANES), :]


def _attn_sample_kernel(q_ref, kn_ref, vn_ref, *refs, n_heads):
    n_groups = n_heads // SUBLANES
    ck_refs, cv_refs = refs[:n_groups], refs[n_groups:2 * n_groups]
    uu_ref, uun_ref, o_ref, qbd_scr, knew_scr, vnew_scr, acc_scr, carry_scr = refs[2 * n_groups:]
    j = pl.program_id(1)
    tnew, d = q_ref.shape

    def all_heads(group_refs):
        return jnp.concatenate([_head_rows(group_refs[h // SUBLANES], h % SUBLANES) for h in range(n_heads)],
                               axis=1).astype(BF16)

    @pl.when(j == 0)
    def _():
        q = q_ref[...]
        lane_head = lax.broadcasted_iota(jnp.int32, (tnew, d), 1) // HEAD_DIM
        for h in range(n_heads):
            qbd_scr[h * tnew:(h + 1) * tnew, :] = jnp.where(lane_head == h, q, jnp.zeros_like(q))
        knew_scr[...] = jnp.zeros_like(knew_scr)
        vnew_scr[...] = jnp.zeros_like(vnew_scr)
        knew_scr[0:tnew, :] = kn_ref[...].astype(BF16)
        vnew_scr[0:tnew, :] = vn_ref[...].astype(BF16)
        z2 = _nt_dot(qbd_scr[...], knew_scr[...])
        t_of_row = lax.broadcasted_iota(jnp.int32, z2.shape, 0) % tnew
        s_of_col = lax.broadcasted_iota(jnp.int32, z2.shape, 1)
        [(pv, rs)] = _sb_blocks([z2], uun_ref, [None], s_of_col < t_of_row,
                                lambda i, w: jnp.dot(w, vnew_scr[...], preferred_element_type=F32))
        acc_scr[...] = pv
        carry_scr[...] = -rs

    @pl.when(j > 0)
    def _():
        z2 = _nt_dot(qbd_scr[...], all_heads(ck_refs))
        vb = all_heads(cv_refs)
        [(pv, rs)] = _sb_blocks([z2], uu_ref, [carry_scr[...]], None,
                                lambda i, w: jnp.dot(w, vb, preferred_element_type=F32))
        acc_scr[...] += pv
        carry_scr[...] -= rs

    @pl.when(j == pl.num_programs(1) - 1)
    def _():
        for h in range(n_heads):
            cols = slice(h * HEAD_DIM, (h + 1) * HEAD_DIM)
            o_ref[:, cols] = acc_scr[h * tnew:(h + 1) * tnew, cols].astype(o_ref.dtype)


def _attn_sample(q, k_new, v_new, cache_k, cache_v):
    b, t, d = q.shape
    n_heads = d // HEAD_DIM
    p, n_groups = cache_k.shape[1:3]
    tk = min(TILE_CACHE, p)
    n_tiles = p // tk
    new_pad = 128
    row_spec = pl.BlockSpec((None, t, d), lambda bi, j: (bi, 0, 0))
    cache_specs = [pl.BlockSpec((None, tk, None, SUBLANES, HEAD_DIM),
                                lambda bi, j, g=g: (bi, n_tiles - jnp.maximum(j, 1), g, 0, 0))
                   for g in range(n_groups)]
    return pl.pallas_call(
        functools.partial(_attn_sample_kernel, n_heads=n_heads),
        grid=(b, n_tiles + 1),
        in_specs=[
            row_spec, row_spec, row_spec, *cache_specs, *cache_specs,
            pl.BlockSpec((2 * tk, tk), lambda bi, j: (0, 0)),
            pl.BlockSpec((2 * new_pad, new_pad), lambda bi, j: (0, 0)),
        ],
        out_specs=row_spec,
        out_shape=jax.ShapeDtypeStruct((b, t, d), BF16),
        scratch_shapes=[
            pltpu.VMEM((n_heads * t, d), BF16),
            pltpu.VMEM((new_pad, d), BF16),
            pltpu.VMEM((new_pad, d), BF16),
            pltpu.VMEM((n_heads * t, d), F32),
            pltpu.VMEM((n_heads * t, 1), F32),
        ],
        compiler_params=_cparams(("parallel", "arbitrary")),
        name="sb_attention_sample",
    )(q, k_new, v_new, *[cache_k] * n_groups, *[cache_v] * n_groups, _later_matrix(tk), _later_matrix(new_pad))


def _proj_res_kernel(a_ref, w_ref, x_ref, gate_ref, o_ref, wb_ref=None):
    bb, tt, k = a_ref.shape
    w_bf = _bf16_weight(w_ref, wb_ref)
    y = jnp.dot(a_ref[...].reshape(bb * tt, k), w_bf[...], preferred_element_type=F32)
    o_ref[...] = x_ref[...] + gate_ref[...] * y.reshape(o_ref.shape)


def _proj_residual(a, w, x, mod, layer, which, boff, bb, tt):
    b, t, k = a.shape
    d = w.shape[1]
    tn = min(TILE_COLS, d)
    grid = (b // bb, t // tt, d // tn)
    emit_w = w.dtype != BF16
    w_spec = pl.BlockSpec((k, tn), lambda bi, i, n: (0, n))
    out_specs = [pl.BlockSpec((bb, tt, tn), lambda bi, i, n: (bi, i, n))]
    out_shape = [jax.ShapeDtypeStruct((b, t, d), F32)]
    if emit_w:
        assert grid[:2] == (1, 1), "each weight tile must be visited exactly once to emit its bf16 copy"
        out_specs.append(w_spec)
        out_shape.append(jax.ShapeDtypeStruct(w.shape, BF16))
    outs = pl.pallas_call(
        _proj_res_kernel,
        grid=grid,
        in_specs=[
            pl.BlockSpec((bb, tt, k), lambda bi, i, n: (bi, i, 0)),
            w_spec,
            pl.BlockSpec((bb, tt, tn), lambda bi, i, n: (bi, i, n)),
            _mod_spec(layer, which, bb, boff, d, ncols=tn),
        ],
        out_specs=out_specs,
        out_shape=out_shape,
        compiler_params=_cparams(("parallel", "parallel", "arbitrary")),
        name="proj_residual",
    )(a, w, x, mod)
    return outs if emit_w else outs[0]


def _ffn_kernel(x_ref, sh_ref, sc_ref, gt_ref, g_ref, wg_ref, wu_ref, wd_ref, o_ref, *rest):
    wgb_ref, wub_ref, wdb_ref, h_scr = rest if len(rest) == 4 else (None, None, None, *rest)
    f = pl.program_id(2)

    @pl.when(f == 0)
    def _():
        h = _normmod(x_ref[...], g_ref[...], sh_ref[...], sc_ref[...])
        h_scr[...] = h.reshape(h_scr.shape).astype(BF16)
        o_ref[...] = jnp.zeros_like(o_ref)

    bb, tt, d = o_ref.shape
    tf = wg_ref.shape[1]
    sub = min(FF_SUBTILE, tf)
    dcols = min(FF_DOWN_COLS, d)
    wg_bf = _bf16_weight(wg_ref, wgb_ref)
    wu_bf = _bf16_weight(wu_ref, wub_ref)
    wd_bf = _bf16_weight(wd_ref, wdb_ref)

    def gate_up(c):
        cols = slice(c * sub, (c + 1) * sub)
        h = h_scr[...]
        return (jnp.dot(h, wg_bf[:, cols], preferred_element_type=F32),
                jnp.dot(h, wu_bf[:, cols], preferred_element_type=F32))

    gate, up = gate_up(0)
    for c in range(tf // sub):
        nxt = gate_up(c + 1) if (c + 1) * sub < tf else None
        a = (gate * jax.nn.sigmoid(gate) * up).astype(BF16)
        for n in range(d // dcols):
            cols = slice(n * dcols, (n + 1) * dcols)
            o_ref[:, :, cols] += jnp.dot(a, wd_bf[c * sub:(c + 1) * sub, cols],
                                         preferred_element_type=F32).reshape(bb, tt, dcols)
        if nxt is not None:
            gate, up = nxt

    @pl.when(f == pl.num_programs(2) - 1)
    def _():
        o_ref[...] = x_ref[...] + gt_ref[...] * o_ref[...]


def _layer_weight_spec(w, layer, block, index_map):
    if w.ndim == 2:
        return pl.BlockSpec(block, index_map)
    return pl.BlockSpec((None, *block), lambda *g: (layer, *index_map(*g)))


def _ffn(x, mod, layer, boff, bb, tt, tf, g_ffn, w_gate, w_up, w_down):
    b, t, d = x.shape
    ff = w_gate.shape[-1]
    grid = (b // bb, t // tt, ff // tf)
    emit_w = w_gate.dtype != BF16
    xspec = pl.BlockSpec((bb, tt, d), lambda bi, i, f: (bi, i, 0))
    up_block, up_map = (d, tf), lambda bi, i, f: (0, f)
    down_block, down_map = (tf, d), lambda bi, i, f: (f, 0)
    out_specs = [xspec]
    out_shape = [jax.ShapeDtypeStruct((b, t, d), F32)]
    if emit_w:
        assert grid[:2] == (1, 1), "each weight tile must be visited exactly once to emit its bf16 copy"
        out_specs += [pl.BlockSpec(up_block, up_map), pl.BlockSpec(up_block, up_map),
                      pl.BlockSpec(down_block, down_map)]
        out_shape += [jax.ShapeDtypeStruct((d, ff), BF16), jax.ShapeDtypeStruct((d, ff), BF16),
                      jax.ShapeDtypeStruct((ff, d), BF16)]
    outs = pl.pallas_call(
        _ffn_kernel,
        grid=grid,
        in_specs=[
            xspec,
            _mod_spec(layer, 3, bb, boff, d),
            _mod_spec(layer, 4, bb, boff, d),
            _mod_spec(layer, 5, bb, boff, d),
            pl.BlockSpec((1, d), lambda bi, i, f: (0, 0)),
            _layer_weight_spec(w_gate, layer, up_block, up_map),
            _layer_weight_spec(w_up, layer, up_block, up_map),
            _layer_weight_spec(w_down, layer, down_block, down_map),
        ],
        out_specs=out_specs,
        out_shape=out_shape,
        scratch_shapes=[pltpu.VMEM((bb * tt, d), BF16)],
        compiler_params=_cparams(("parallel", "parallel", "arbitrary")),
        name="swiglu_ffn",
    )(x, mod, mod, mod, g_ffn, w_gate, w_up, w_down)
    return outs if emit_w else outs[0]


def _pool_kernel(x_ref, halo_ref, sh_ref, sc_ref, gt_ref, g_ref, wp_ref, ps_ref, o_ref, tail_ref,
                 *, halo_is_x, n_hist):
    i = pl.program_id(1)
    bb, tt, d = x_ref.shape
    dg = d // len(POOL_WINDOWS)
    seg = HALO + tt
    x = x_ref[...]
    h = _normmod(x, g_ref[...], sh_ref[...], sc_ref[...])
    if halo_is_x:
        hist = _normmod(halo_ref[...], g_ref[...], sh_ref[...], sc_ref[...])
        hist = jnp.where(i > 0, hist, 0.0)
    else:
        hist = halo_ref[...]
    tail_ref[...] = h[:, tt - HALO:, :]

    s_all = jnp.concatenate([hist, h], axis=1).reshape(bb * seg, d)
    h2 = h.reshape(bb * tt, d)
    pos = i * tt + lax.broadcasted_iota(jnp.int32, (bb * tt, 1), 0) % tt
    outs = []
    for g, win in enumerate(POOL_WINDOWS):
        cols = slice(g * dg, (g + 1) * dg)
        s = s_all[:, cols]
        span = 1
        while span < win:
            s = s + pltpu.roll(s, span, 0)
            span *= 2
        s = s.reshape(bb, seg, dg)[:, HALO:, :].reshape(bb * tt, dg)
        count = jnp.minimum(n_hist + pos + 1, win).astype(F32)
        diff = (s / count - h2[:, cols]).astype(BF16)
        outs.append(jnp.dot(diff, wp_ref[g], preferred_element_type=F32))
    mix = (jnp.concatenate(outs, axis=1) * ps_ref[...]).reshape(bb, tt, d)
    o_ref[...] = x + gt_ref[...] * mix


def _pool_layer(x, halo, halo_is_x, n_hist, mod, layer, boff, bb, tt, g_mix, w_pool, pool_scale):
    b, t, d = x.shape
    dg = d // len(POOL_WINDOWS)
    if halo_is_x:
        per = tt // HALO
        halo_spec = pl.BlockSpec((bb, HALO, d), lambda bi, i: (bi, jnp.maximum(i * per - 1, 0), 0))
    else:
        halo_spec = pl.BlockSpec((bb, HALO, d), lambda bi, i: (bi, 0, 0))
    xspec = pl.BlockSpec((bb, tt, d), lambda bi, i: (bi, i, 0))
    return pl.pallas_call(
        functools.partial(_pool_kernel, halo_is_x=halo_is_x, n_hist=n_hist),
        grid=(b // bb, t // tt),
        in_specs=[
            xspec, halo_spec,
            _mod_spec(layer, 0, bb, boff, d),
            _mod_spec(layer, 1, bb, boff, d),
            _mod_spec(layer, 2, bb, boff, d),
            pl.BlockSpec((1, d), lambda bi, i: (0, 0)),
            pl.BlockSpec((len(POOL_WINDOWS), dg, dg), lambda bi, i: (0, 0, 0)),
            pl.BlockSpec((1, d), lambda bi, i: (0, 0)),
        ],
        out_specs=[xspec, pl.BlockSpec((bb, HALO, d), lambda bi, i: (bi, 0, 0))],
        out_shape=[jax.ShapeDtypeStruct((b, t, d), F32), jax.ShapeDtypeStruct((b, HALO, d), F32)],
        compiler_params=_cparams(("parallel", "arbitrary")),
        name="pool_mixer",
    )(x, halo, mod, mod, mod, g_mix, w_pool, pool_scale)


class _Tiles(NamedTuple):
    rows: int
    qkv_rows: int
    qkv_cols: int
    ffn_rows: int
    ffn_cols: int
    pool_rows: int


def _as_list(x):
    return list(x) if isinstance(x, (list, tuple)) else [x]


def _trunk(x, mod, boff, bb, tiles, caches, weights):
    (g_mix, g_ffn, w_qkv, g_q, g_k, w_o, w_pool, pool_scale, ffn_weights) = weights
    b, t, d = x.shape
    n_heads = d // HEAD_DIM
    emit_w = w_qkv.dtype != BF16
    q, k, v, k_bf, v_bf, *w_qkv_bf = _qkv_project(x, mod, boff, bb, tiles.qkv_rows, tiles.qkv_cols,
                                                  g_mix[0:1], w_qkv, g_q, g_k)
    if caches is None:
        o = _attn_prompt(q, k_bf, v_bf)
    else:
        o = _attn_sample(q, k, v, caches[0], caches[1])
    x, *w_o_bf = _as_list(_proj_residual(o, w_o, x, mod, 0, 2, boff, bb, tiles.rows))
    x, *ffn0_bf = _as_list(_ffn(x, mod, 0, boff, bb, tiles.ffn_rows, tiles.ffn_cols, g_ffn[0:1], *ffn_weights[0]))
    if caches is None:
        x, tail = _pool_layer(x, x, True, 0, mod, 1, boff, bb, tiles.pool_rows, g_mix[1:2], w_pool, pool_scale)
    else:
        hist = jnp.pad(caches[2], ((0, 0), (1, 0), (0, 0)))
        x, tail = _pool_layer(x, hist, False, caches[0].shape[1], mod, 1, boff, bb, tiles.pool_rows,
                              g_mix[1:2], w_pool, pool_scale)
    x, *ffn1_bf = _as_list(_ffn(x, mod, 1, boff, bb, tiles.ffn_rows, tiles.ffn_cols, g_ffn[1:2], *ffn_weights[1]))
    heads = (1, b, t, n_heads, HEAD_DIM)
    outs = (x, k.reshape(heads), v.reshape(heads), tail[None, :, 1:, :])
    if not emit_w:
        return outs, None
    return outs, (g_mix, g_ffn, w_qkv_bf[0], g_q, g_k, w_o_bf[0], w_pool, pool_scale, (ffn0_bf, ffn1_bf))


def kernel(x_prompt, x_sample, c_prompt, c_sample, cache_k, cache_v, state_pool, w_ada, b_ada, g_mix, g_ffn,
           w_qkv, g_q, g_k, w_o, w_pool, pool_scale, w_gate, w_up, w_down):
    bp, tp, d = x_prompt.shape
    bs, ts, _ = x_sample.shape
    past = cache_k.shape[2]
    ff = w_gate.shape[2]
    assert w_ada.shape[0] == 2 and cache_k.shape[0] == 1 and state_pool.shape[0] == 1 and w_qkv.shape[0] == 1
    assert state_pool.shape[2] == POOL_STATE and ts % HALO == 0 and d % (HEAD_DIM * len(POOL_WINDOWS)) == 0

    mod = _ada_modulation(jnp.concatenate([c_sample, c_prompt], axis=0), w_ada, b_ada)

    ffn_stacks = (w_gate, w_up, w_down)
    weights = (g_mix, g_ffn, w_qkv[0], g_q, g_k, w_o[0], w_pool[0].astype(BF16), pool_scale,
               (ffn_stacks, ffn_stacks))
    cache_shape = (bs, past, d // HEAD_DIM // SUBLANES, SUBLANES, HEAD_DIM)
    caches = (cache_k.reshape(cache_shape), cache_v.reshape(cache_shape), state_pool[0])
    sample_tiles = _Tiles(rows=ts, qkv_rows=ts, qkv_cols=min(TILE_QKV_COLS_F32, d), ffn_rows=ts,
                          ffn_cols=min(TILE_FF_F32, ff), pool_rows=ts)
    (y_s, k_s, v_s, pool_s), weights_bf = _trunk(x_sample, mod, 0, bs, sample_tiles, caches, weights)

    prompt_tiles = _Tiles(rows=min(TILE_ROWS, tp), qkv_rows=min(TILE_QKV_ROWS, tp), qkv_cols=min(TILE_QKV_COLS, d),
                          ffn_rows=min(TILE_FFN_ROWS, tp), ffn_cols=min(TILE_FF, ff),
                          pool_rows=min(TILE_POOL_ROWS, tp))
    (y_p, k_p, v_p, pool_p), _ = _trunk(x_prompt, mod, bs, 1, prompt_tiles, None,
                                        weights if weights_bf is None else weights_bf)
    return (y_p, y_s, k_p, v_p, pool_p, k_s, v_s, pool_s)
```

```python
import functools
from typing import NamedTuple

import jax
import jax.numpy as jnp
from jax import lax
from jax.experimental import pallas as pl
from jax.experimental.pallas import tpu as pltpu

F32 = jnp.float32
BF16 = jnp.bfloat16

HEAD_DIM = 128
SUBLANES = 8
EPS = 1e-6
N_MOD = 6
POOL_WINDOWS = (2, 4, 8, 16)
HALO = 16
POOL_STATE = HALO - 1

V7X_VMEM_BYTES = 64 * 1024 * 1024
VMEM_LIMIT = V7X_VMEM_BYTES - 8 * 1024 * 1024

TILE_ROWS = 1024
TILE_COLS = 1024
TILE_QKV_ROWS = 1024
TILE_QKV_COLS = 512
ROW_SPLIT = 4
TILE_QKV_COLS_F32 = 512
QKV_SUBTILE = 256
FF_DOWN_COLS = 512
TILE_FFN_ROWS = 512
TILE_FF = 512
TILE_FF_F32 = 256
FF_SUBTILE = 256
TILE_POOL_ROWS = 512
TILE_Q = 256
LOGITS_AHEAD = 1
HEADS_PER_STEP = 8
TILE_CACHE = 512
CACHE_SPLIT = 4
TILE_ADA = 1024


def _cparams(sem):
    return pltpu.CompilerParams(dimension_semantics=sem, vmem_limit_bytes=VMEM_LIMIT)


def _normmod(x3, g, shift, scale):
    ms = jnp.mean(x3 * x3, axis=-1, keepdims=True)
    y = x3 * lax.rsqrt(ms + EPS) * g
    return y * (1.0 + scale) + shift


def _later_matrix(n):
    j = jnp.arange(n)[:, None]
    s = jnp.arange(n)[None, :]
    u = (j >= s).astype(BF16)
    return jnp.concatenate([u, u], axis=0)


LOG2_E = 1.4426950408889634
INV_LN2 = LOG2_E
SOFTPLUS2_CLAMP = 64.0
Q_PRESCALE = HEAD_DIM ** -0.5 * LOG2_E


def _sb_blocks(z2_fn, n_chains, uu_ref, carries, mask, pv_fn):
    z2s = [z2_fn(i) for i in range(min(LOGITS_AHEAD, n_chains))]
    sums, rowsums = [], []
    for i in range(n_chains):
        if i + LOGITS_AHEAD < n_chains:
            z2s.append(z2_fn(i + LOGITS_AHEAD))
        z2 = z2s[i]
        sp = jnp.maximum(z2, jnp.log(1.0 + jnp.exp2(jnp.minimum(z2, SOFTPLUS2_CLAMP))) * INV_LN2)
        if mask is not None:
            sp = jnp.where(mask, sp, 0.0)
        hi = sp.astype(BF16)
        lo = (sp - hi.astype(F32)).astype(BF16)
        sums.append(jnp.dot(jnp.concatenate([hi, lo], axis=1), uu_ref[...], preferred_element_type=F32))
        rowsums.append(jnp.sum(sp, axis=1, keepdims=True))
    outs = []
    for i, (z2, total, rs, carry) in enumerate(zip(z2s, sums, rowsums, carries)):
        arg = z2 - total
        if carry is not None:
            arg = arg + carry
        w = jnp.exp2(arg)
        if mask is not None:
            w = jnp.where(mask, w, 0.0)
        outs.append((pv_fn(i, w.astype(BF16)), rs))
    return outs


def _ada_kernel(c_ref, w_ref, b_ref, o_ref):
    c = c_ref[...]
    s = (c * jax.nn.sigmoid(c)).astype(BF16)
    o_ref[...] = jnp.dot(s, w_ref[...].astype(BF16), preferred_element_type=F32) + b_ref[...]


def _ada_modulation(c_all, w_ada, b_ada):
    n_layers, d, _ = w_ada.shape
    bt = c_all.shape[0]
    tn = min(TILE_ADA, d)
    npc = d // tn
    out = pl.pallas_call(
        _ada_kernel,
        grid=(n_layers, N_MOD * npc),
        in_specs=[
            pl.BlockSpec((bt, d), lambda l, n: (0, 0)),
            pl.BlockSpec((None, d, tn), lambda l, n: (l, 0, n)),
            pl.BlockSpec((None, 1, tn), lambda l, n: (l, 0, n)),
        ],
        out_specs=pl.BlockSpec((None, None, bt, tn), lambda l, n: (l, n // npc, 0, n % npc)),
        out_shape=jax.ShapeDtypeStruct((n_layers, N_MOD, bt, d), F32),
        compiler_params=_cparams(("arbitrary", "arbitrary")),
        name="ada_modulation",
    )(c_all, w_ada, b_ada.reshape(n_layers, 1, N_MOD * d))
    return out.reshape(n_layers, N_MOD, bt, 1, d)


def _mod_spec(layer, which, bb, boff, d, ncols=None):
    if ncols is None:
        return pl.BlockSpec((None, None, bb, 1, d), lambda b, i, *_: (layer, which, boff // bb + b, 0, 0))
    return pl.BlockSpec((None, None, bb, 1, ncols), lambda b, i, n: (layer, which, boff // bb + b, 0, n))


def _head_norm(y, g):
    outs = []
    for h in range(y.shape[1] // HEAD_DIM):
        yh = y[:, h * HEAD_DIM:(h + 1) * HEAD_DIM]
        ms = jnp.mean(yh * yh, axis=-1, keepdims=True)
        outs.append(yh * lax.rsqrt(ms + EPS) * g)
    return outs[0] if len(outs) == 1 else jnp.concatenate(outs, axis=1)


def _bf16_weight(w_ref, wb_ref):
    if wb_ref is None:
        return w_ref
    wb_ref[...] = w_ref[...].astype(BF16)
    return wb_ref


def _row_split(bb, tt):
    return ROW_SPLIT if bb == 1 and tt % (ROW_SPLIT * SUBLANES) == 0 else 1


def _row_split_specs(bb, tt, d):
    n = _row_split(bb, tt)
    return [pl.BlockSpec((bb, tt // n, d), lambda bi, i, *_, s=s: (bi, i * n + s, 0)) for s in range(n)]


def _normmod_rows(x_refs, g_ref, sh_ref, sc_ref, h_scr):
    rows = h_scr.shape[0] // len(x_refs)
    for s, x_ref in enumerate(x_refs):
        h = _normmod(x_ref[...], g_ref[...], sh_ref[...], sc_ref[...])
        h_scr[s * rows:(s + 1) * rows, :] = h.reshape(rows, h_scr.shape[1]).astype(BF16)


def _qkv_kernel(*refs, n_per, n_x):
    x_refs = refs[:n_x]
    sh_ref, sc_ref, g_ref, w_ref, gq_ref, gk_ref, q_ref, k_ref, v_ref, kb_ref, vb_ref, *rest = refs[n_x:]
    wb_ref, h_scr = rest if len(rest) == 2 else (None, *rest)
    j = pl.program_id(2)

    @pl.when(j == 0)
    def _():
        _normmod_rows(x_refs, g_ref, sh_ref, sc_ref, h_scr)

    bb, tt, tn = q_ref.shape
    sub = min(QKV_SUBTILE, tn)
    w_bf = _bf16_weight(w_ref, wb_ref)

    def project(c):
        return jnp.dot(h_scr[...], w_bf[:, c * sub:(c + 1) * sub], preferred_element_type=F32)

    def pipelined(epilogue):
        y = project(0)
        for c in range(tn // sub):
            nxt = project(c + 1) if (c + 1) * sub < tn else None
            epilogue(y, slice(c * sub, (c + 1) * sub))
            y = nxt

    def store_q(y, cols):
        q_ref[:, :, cols] = _head_norm(y, gq_ref[...] * Q_PRESCALE).astype(BF16).reshape(bb, tt, sub)

    def store_k(y, cols):
        k = _head_norm(y, gk_ref[...])
        k_ref[:, :, cols] = k.reshape(bb, tt, sub)
        kb_ref[:, :, cols] = k.astype(BF16).reshape(bb, tt, sub)

    def store_v(y, cols):
        v_ref[:, :, cols] = y.reshape(bb, tt, sub)
        vb_ref[:, :, cols] = y.astype(BF16).reshape(bb, tt, sub)

    pl.when(j < n_per)(lambda: pipelined(store_q))
    pl.when(jnp.logical_and(j >= n_per, j < 2 * n_per))(lambda: pipelined(store_k))
    pl.when(j >= 2 * n_per)(lambda: pipelined(store_v))


def _qkv_project(x, mod, boff, bb, tt, tn, g_mix, w_qkv, g_q, g_k):
    b, t, d = x.shape
    n_per = d // tn
    emit_w = w_qkv.dtype != BF16
    grid = (b // bb, t // tt, 3 * n_per)
    q_spec = pl.BlockSpec((bb, tt, tn), lambda bi, i, j: (bi, i, jnp.minimum(j, n_per - 1)))
    k_spec = pl.BlockSpec((bb, tt, tn), lambda bi, i, j: (bi, i, jnp.clip(j - n_per, 0, n_per - 1)))
    v_spec = pl.BlockSpec((bb, tt, tn), lambda bi, i, j: (bi, i, jnp.maximum(j - 2 * n_per, 0)))
    w_spec = pl.BlockSpec((d, tn), lambda bi, i, j: (0, j))
    out_specs = [q_spec, k_spec, v_spec, k_spec, v_spec]
    out_shape = [
        jax.ShapeDtypeStruct((b, t, d), BF16),
        jax.ShapeDtypeStruct((b, t, d), F32),
        jax.ShapeDtypeStruct((b, t, d), F32),
        jax.ShapeDtypeStruct((b, t, d), BF16),
        jax.ShapeDtypeStruct((b, t, d), BF16),
    ]
    if emit_w:
        assert grid[:2] == (1, 1), "each weight tile must be visited exactly once to emit its bf16 copy"
        out_specs.append(w_spec)
        out_shape.append(jax.ShapeDtypeStruct(w_qkv.shape, BF16))
    x_specs = _row_split_specs(bb, tt, d)
    return pl.pallas_call(
        functools.partial(_qkv_kernel, n_per=n_per, n_x=len(x_specs)),
        grid=grid,
        in_specs=[
            *x_specs,
            _mod_spec(0, 0, bb, boff, d),
            _mod_spec(0, 1, bb, boff, d),
            pl.BlockSpec((1, d), lambda bi, i, j: (0, 0)),
            w_spec,
            pl.BlockSpec((1, HEAD_DIM), lambda bi, i, j: (0, 0)),
            pl.BlockSpec((1, HEAD_DIM), lambda bi, i, j: (0, 0)),
        ],
        out_specs=out_specs,
        out_shape=out_shape,
        scratch_shapes=[pltpu.VMEM((bb * tt, d), BF16)],
        compiler_params=_cparams(("parallel", "parallel", "arbitrary")),
        name="qkv_project",
    )(*[x] * len(x_specs), mod, mod, g_mix, w_qkv, g_q, g_k)


def _nt_dot(a, b):
    return lax.dot_general(a, b, (((1,), (1,)), ((), ())), preferred_element_type=F32)


def _attn_prompt_kernel(q_ref, k_ref, v_ref, uu_ref, o_ref, acc_scr, carry_scr, *, tq, n_par):
    t = q_ref.shape[0]
    row = lax.broadcasted_iota(jnp.int32, (tq, tq), 0)
    col = lax.broadcasted_iota(jnp.int32, (tq, tq), 1)
    causal = col < row
    heads = [slice(h * HEAD_DIM, (h + 1) * HEAD_DIM) for h in range(n_par)]

    def blocks(qrows, krows, first):
        carries = [None if first else carry_scr[i] for i in range(n_par)]
        res = _sb_blocks(lambda i: _nt_dot(q_ref[qrows, heads[i]], k_ref[krows, heads[i]]), n_par,
                         uu_ref, carries, causal if first else None,
                         lambda i, w: jnp.dot(w, v_ref[krows, heads[i]], preferred_element_type=F32))
        for i, (pv, rs) in enumerate(res):
            if first:
                acc_scr[i] = pv
                carry_scr[i] = -rs
            else:
                acc_scr[i] += pv
                carry_scr[i] -= rs

    def query_block(qi, _):
        qrows = pl.ds(pl.multiple_of(qi * tq, tq), tq)
        blocks(qrows, qrows, True)

        def body(step, _):
            blocks(qrows, pl.ds(pl.multiple_of((qi - 1 - step) * tq, tq), tq), False)
            return 0

        lax.fori_loop(0, qi, body, 0)
        for i, hc in enumerate(heads):
            o_ref[qrows, hc] = acc_scr[i].astype(o_ref.dtype)
        return 0

    lax.fori_loop(0, t // tq, query_block, 0)


def _attn_prompt(q, k, v):
    b, t, d = q.shape
    tq = min(TILE_Q, t)
    n_par = HEADS_PER_STEP
    width = n_par * HEAD_DIM
    spec = pl.BlockSpec((None, t, width), lambda bi, h: (bi, 0, h))
    return pl.pallas_call(
        functools.partial(_attn_prompt_kernel, tq=tq, n_par=n_par),
        grid=(b, d // width),
        in_specs=[spec, spec, spec, pl.BlockSpec((2 * tq, tq), lambda bi, h: (0, 0))],
        out_specs=spec,
        out_shape=jax.ShapeDtypeStruct((b, t, d), BF16),
        scratch_shapes=[pltpu.VMEM((n_par, tq, HEAD_DIM), F32), pltpu.VMEM((n_par, tq, 1), F32)],
        compiler_params=_cparams(("parallel", "parallel")),
        name="sb_attention_prompt",
    )(q, k, v, _later_matrix(tq))


def _head_rows(group_ref, i):
    tk = group_ref.shape[0]
    return group_ref.reshape(tk * SUBLANES, HEAD_DIM)[pl.ds(i, tk, stride=SUBLANES), :]


def _attn_sample_kernel(q_ref, kn_ref, vn_ref, *refs, n_heads):
    n_groups = n_heads // SUBLANES
    n_parts = n_groups * CACHE_SPLIT
    ck_refs, cv_refs = refs[:n_parts], refs[n_parts:2 * n_parts]
    uu_ref, uun_ref, o_ref, qbd_scr, knew_scr, vnew_scr, acc_scr, carry_scr = refs[2 * n_parts:]
    j = pl.program_id(1)
    tnew, d = q_ref.shape

    def all_heads(part_refs):
        def head(h):
            g, i = divmod(h, SUBLANES)
            return jnp.concatenate([_head_rows(part_refs[s * n_groups + g], i) for s in range(CACHE_SPLIT)],
                                   axis=0)
        return jnp.concatenate([head(h) for h in range(n_heads)], axis=1).astype(BF16)

    @pl.when(j == 0)
    def _():
        q = q_ref[...]
        lane_head = lax.broadcasted_iota(jnp.int32, (tnew, d), 1) // HEAD_DIM
        for h in range(n_heads):
            qbd_scr[h * tnew:(h + 1) * tnew, :] = jnp.where(lane_head == h, q, jnp.zeros_like(q))
        knew_scr[...] = jnp.zeros_like(knew_scr)
        vnew_scr[...] = jnp.zeros_like(vnew_scr)
        knew_scr[0:tnew, :] = kn_ref[...].astype(BF16)
        vnew_scr[0:tnew, :] = vn_ref[...].astype(BF16)
        z2 = _nt_dot(qbd_scr[...], knew_scr[...])
        t_of_row = lax.broadcasted_iota(jnp.int32, z2.shape, 0) % tnew
        s_of_col = lax.broadcasted_iota(jnp.int32, z2.shape, 1)
        [(pv, rs)] = _sb_blocks(lambda i: z2, 1, uun_ref, [None], s_of_col < t_of_row,
                                lambda i, w: jnp.dot(w, vnew_scr[...], preferred_element_type=F32))
        acc_scr[...] = pv
        carry_scr[...] = -rs

    @pl.when(j > 0)
    def _():
        z2 = _nt_dot(qbd_scr[...], all_heads(ck_refs))
        vb = all_heads(cv_refs)
        [(pv, rs)] = _sb_blocks(lambda i: z2, 1, uu_ref, [carry_scr[...]], None,
                                lambda i, w: jnp.dot(w, vb, preferred_element_type=F32))
        acc_scr[...] += pv
        carry_scr[...] -= rs

    @pl.when(j == pl.num_programs(1) - 1)
    def _():
        for h in range(n_heads):
            cols = slice(h * HEAD_DIM, (h + 1) * HEAD_DIM)
            o_ref[:, cols] = acc_scr[h * tnew:(h + 1) * tnew, cols].astype(o_ref.dtype)


def _attn_sample(q, k_new, v_new, cache_k, cache_v):
    b, t, d = q.shape
    n_heads = d // HEAD_DIM
    p, n_groups = cache_k.shape[1:3]
    tk = min(TILE_CACHE, p)
    n_tiles = p // tk
    new_pad = 128
    row_spec = pl.BlockSpec((None, t, d), lambda bi, j: (bi, 0, 0))
    part = tk // CACHE_SPLIT
    cache_specs = [pl.BlockSpec((None, part, None, SUBLANES, HEAD_DIM),
                                lambda bi, j, s=s, g=g: (bi, (n_tiles - jnp.maximum(j, 1)) * CACHE_SPLIT + s, g, 0, 0))
                   for s in range(CACHE_SPLIT) for g in range(n_groups)]
    return pl.pallas_call(
        functools.partial(_attn_sample_kernel, n_heads=n_heads),
        grid=(b, n_tiles + 1),
        in_specs=[
            row_spec, row_spec, row_spec, *cache_specs, *cache_specs,
            pl.BlockSpec((2 * tk, tk), lambda bi, j: (0, 0)),
            pl.BlockSpec((2 * new_pad, new_pad), lambda bi, j: (0, 0)),
        ],
        out_specs=row_spec,
        out_shape=jax.ShapeDtypeStruct((b, t, d), BF16),
        scratch_shapes=[
            pltpu.VMEM((n_heads * t, d), BF16),
            pltpu.VMEM((new_pad, d), BF16),
            pltpu.VMEM((new_pad, d), BF16),
            pltpu.VMEM((n_heads * t, d), F32),
            pltpu.VMEM((n_heads * t, 1), F32),
        ],
        compiler_params=_cparams(("parallel", "arbitrary")),
        name="sb_attention_sample",
    )(q, k_new, v_new, *[cache_k] * len(cache_specs), *[cache_v] * len(cache_specs),
      _later_matrix(tk), _later_matrix(new_pad))


def _proj_res_kernel(a_ref, w_ref, x_ref, gate_ref, o_ref, wb_ref=None):
    bb, tt, k = a_ref.shape
    w_bf = _bf16_weight(w_ref, wb_ref)
    y = jnp.dot(a_ref[...].reshape(bb * tt, k), w_bf[...], preferred_element_type=F32)
    o_ref[...] = x_ref[...] + gate_ref[...] * y.reshape(o_ref.shape)


def _proj_residual(a, w, x, mod, layer, which, boff, bb, tt):
    b, t, k = a.shape
    d = w.shape[1]
    tn = min(TILE_COLS, d)
    grid = (b // bb, t // tt, d // tn)
    emit_w = w.dtype != BF16
    w_spec = pl.BlockSpec((k, tn), lambda bi, i, n: (0, n))
    out_specs = [pl.BlockSpec((bb, tt, tn), lambda bi, i, n: (bi, i, n))]
    out_shape = [jax.ShapeDtypeStruct((b, t, d), F32)]
    if emit_w:
        assert grid[:2] == (1, 1), "each weight tile must be visited exactly once to emit its bf16 copy"
        out_specs.append(w_spec)
        out_shape.append(jax.ShapeDtypeStruct(w.shape, BF16))
    outs = pl.pallas_call(
        _proj_res_kernel,
        grid=grid,
        in_specs=[
            pl.BlockSpec((bb, tt, k), lambda bi, i, n: (bi, i, 0)),
            w_spec,
            pl.BlockSpec((bb, tt, tn), lambda bi, i, n: (bi, i, n)),
            _mod_spec(layer, which, bb, boff, d, ncols=tn),
        ],
        out_specs=out_specs,
        out_shape=out_shape,
        compiler_params=_cparams(("parallel", "parallel", "arbitrary")),
        name="proj_residual",
    )(a, w, x, mod)
    return outs if emit_w else outs[0]


def _ffn_kernel(x_ref, sh_ref, sc_ref, gt_ref, g_ref, wg_ref, wu_ref, wd_ref, o_ref, *rest):
    wgb_ref, wub_ref, wdb_ref, h_scr = rest if len(rest) == 4 else (None, None, None, *rest)
    f = pl.program_id(2)

    @pl.when(f == 0)
    def _():
        h = _normmod(x_ref[...], g_ref[...], sh_ref[...], sc_ref[...])
        h_scr[...] = h.reshape(h_scr.shape).astype(BF16)
        o_ref[...] = jnp.zeros_like(o_ref)

    bb, tt, d = o_ref.shape
    tf = wg_ref.shape[1]
    sub = min(FF_SUBTILE, tf)
    dcols = min(FF_DOWN_COLS, d)
    wg_bf = _bf16_weight(wg_ref, wgb_ref)
    wu_bf = _bf16_weight(wu_ref, wub_ref)
    wd_bf = _bf16_weight(wd_ref, wdb_ref)

    def gate_up(c):
        cols = slice(c * sub, (c + 1) * sub)
        h = h_scr[...]
        return (jnp.dot(h, wg_bf[:, cols], preferred_element_type=F32),
                jnp.dot(h, wu_bf[:, cols], preferred_element_type=F32))

    gate, up = gate_up(0)
    for c in range(tf // sub):
        nxt = gate_up(c + 1) if (c + 1) * sub < tf else None
        a = (gate * jax.nn.sigmoid(gate) * up).astype(BF16)
        for n in range(d // dcols):
            cols = slice(n * dcols, (n + 1) * dcols)
            o_ref[:, :, cols] += jnp.dot(a, wd_bf[c * sub:(c + 1) * sub, cols],
                                         preferred_element_type=F32).reshape(bb, tt, dcols)
        if nxt is not None:
            gate, up = nxt

    @pl.when(f == pl.num_programs(2) - 1)
    def _():
        o_ref[...] = x_ref[...] + gt_ref[...] * o_ref[...]


def _layer_weight_spec(w, layer, block, index_map):
    if w.ndim == 2:
        return pl.BlockSpec(block, index_map)
    return pl.BlockSpec((None, *block), lambda *g: (layer, *index_map(*g)))


def _ffn(x, mod, layer, boff, bb, tt, tf, g_ffn, w_gate, w_up, w_down):
    b, t, d = x.shape
    ff = w_gate.shape[-1]
    grid = (b // bb, t // tt, ff // tf)
    emit_w = w_gate.dtype != BF16
    xspec = pl.BlockSpec((bb, tt, d), lambda bi, i, f: (bi, i, 0))
    up_block, up_map = (d, tf), lambda bi, i, f: (0, f)
    down_block, down_map = (tf, d), lambda bi, i, f: (f, 0)
    out_specs = [xspec]
    out_shape = [jax.ShapeDtypeStruct((b, t, d), F32)]
    if emit_w:
        assert grid[:2] == (1, 1), "each weight tile must be visited exactly once to emit its bf16 copy"
        out_specs += [pl.BlockSpec(up_block, up_map), pl.BlockSpec(up_block, up_map),
                      pl.BlockSpec(down_block, down_map)]
        out_shape += [jax.ShapeDtypeStruct((d, ff), BF16), jax.ShapeDtypeStruct((d, ff), BF16),
                      jax.ShapeDtypeStruct((ff, d), BF16)]
    outs = pl.pallas_call(
        _ffn_kernel,
        grid=grid,
        in_specs=[
            xspec,
            _mod_spec(layer, 3, bb, boff, d),
            _mod_spec(layer, 4, bb, boff, d),
            _mod_spec(layer, 5, bb, boff, d),
            pl.BlockSpec((1, d), lambda bi, i, f: (0, 0)),
            _layer_weight_spec(w_gate, layer, up_block, up_map),
            _layer_weight_spec(w_up, layer, up_block, up_map),
            _layer_weight_spec(w_down, layer, down_block, down_map),
        ],
        out_specs=out_specs,
        out_shape=out_shape,
        scratch_shapes=[pltpu.VMEM((bb * tt, d), BF16)],
        compiler_params=_cparams(("parallel", "parallel", "arbitrary")),
        name="swiglu_ffn",
    )(x, mod, mod, mod, g_ffn, w_gate, w_up, w_down)
    return outs if emit_w else outs[0]


def _pool_kernel(x_ref, halo_ref, sh_ref, sc_ref, gt_ref, g_ref, wp_ref, ps_ref, o_ref, tail_ref,
                 *, halo_is_x, n_hist):
    i = pl.program_id(1)
    bb, tt, d = x_ref.shape
    dg = d // len(POOL_WINDOWS)
    seg = HALO + tt
    x = x_ref[...]
    h = _normmod(x, g_ref[...], sh_ref[...], sc_ref[...])
    if halo_is_x:
        hist = _normmod(halo_ref[...], g_ref[...], sh_ref[...], sc_ref[...])
        hist = jnp.where(i > 0, hist, 0.0)
    else:
        hist = halo_ref[...]
    tail_ref[...] = h[:, tt - HALO:, :]

    s_all = jnp.concatenate([hist, h], axis=1).reshape(bb * seg, d)
    h2 = h.reshape(bb * tt, d)
    pos = i * tt + lax.broadcasted_iota(jnp.int32, (bb * tt, 1), 0) % tt
    outs = []
    for g, win in enumerate(POOL_WINDOWS):
        cols = slice(g * dg, (g + 1) * dg)
        s = s_all[:, cols]
        span = 1
        while span < win:
            s = s + pltpu.roll(s, span, 0)
            span *= 2
        s = s.reshape(bb, seg, dg)[:, HALO:, :].reshape(bb * tt, dg)
        count = jnp.minimum(n_hist + pos + 1, win).astype(F32)
        diff = (s / count - h2[:, cols]).astype(BF16)
        outs.append(jnp.dot(diff, wp_ref[g], preferred_element_type=F32))
    mix = (jnp.concatenate(outs, axis=1) * ps_ref[...]).reshape(bb, tt, d)
    o_ref[...] = x + gt_ref[...] * mix


def _pool_layer(x, halo, halo_is_x, n_hist, mod, layer, boff, bb, tt, g_mix, w_pool, pool_scale):
    b, t, d = x.shape
    dg = d // len(POOL_WINDOWS)
    if halo_is_x:
        per = tt // HALO
        halo_spec = pl.BlockSpec((bb, HALO, d), lambda bi, i: (bi, jnp.maximum(i * per - 1, 0), 0))
    else:
        halo_spec = pl.BlockSpec((bb, HALO, d), lambda bi, i: (bi, 0, 0))
    xspec = pl.BlockSpec((bb, tt, d), lambda bi, i: (bi, i, 0))
    return pl.pallas_call(
        functools.partial(_pool_kernel, halo_is_x=halo_is_x, n_hist=n_hist),
        grid=(b // bb, t // tt),
        in_specs=[
            xspec, halo_spec,
            _mod_spec(layer, 0, bb, boff, d),
            _mod_spec(layer, 1, bb, boff, d),
            _mod_spec(layer, 2, bb, boff, d),
            pl.BlockSpec((1, d), lambda bi, i: (0, 0)),
            pl.BlockSpec((len(POOL_WINDOWS), dg, dg), lambda bi, i: (0, 0, 0)),
            pl.BlockSpec((1, d), lambda bi, i: (0, 0)),
        ],
        out_specs=[xspec, pl.BlockSpec((bb, HALO, d), lambda bi, i: (bi, 0, 0))],
        out_shape=[jax.ShapeDtypeStruct((b, t, d), F32), jax.ShapeDtypeStruct((b, HALO, d), F32)],
        compiler_params=_cparams(("parallel", "arbitrary")),
        name="pool_mixer",
    )(x, halo, mod, mod, mod, g_mix, w_pool, pool_scale)


class _Tiles(NamedTuple):
    rows: int
    qkv_rows: int
    qkv_cols: int
    ffn_rows: int
    ffn_cols: int
    pool_rows: int


def _as_list(x):
    return list(x) if isinstance(x, (list, tuple)) else [x]


def _trunk(x, mod, boff, bb, tiles, caches, weights):
    (g_mix, g_ffn, w_qkv, g_q, g_k, w_o, w_pool, pool_scale, ffn_weights) = weights
    b, t, d = x.shape
    n_heads = d // HEAD_DIM
    emit_w = w_qkv.dtype != BF16
    q, k, v, k_bf, v_bf, *w_qkv_bf = _qkv_project(x, mod, boff, bb, tiles.qkv_rows, tiles.qkv_cols,
                                                  g_mix[0:1], w_qkv, g_q, g_k)
    if caches is None:
        o = _attn_prompt(q, k_bf, v_bf)
    else:
        o = _attn_sample(q, k, v, caches[0], caches[1])
    x, *w_o_bf = _as_list(_proj_residual(o, w_o, x, mod, 0, 2, boff, bb, tiles.rows))
    x, *ffn0_bf = _as_list(_ffn(x, mod, 0, boff, bb, tiles.ffn_rows, tiles.ffn_cols, g_ffn[0:1], *ffn_weights[0]))
    if caches is None:
        x, tail = _pool_layer(x, x, True, 0, mod, 1, boff, bb, tiles.pool_rows, g_mix[1:2], w_pool, pool_scale)
    else:
        hist = jnp.pad(caches[2], ((0, 0), (1, 0), (0, 0)))
        x, tail = _pool_layer(x, hist, False, caches[0].shape[1], mod, 1, boff, bb, tiles.pool_rows,
                              g_mix[1:2], w_pool, pool_scale)
    x, *ffn1_bf = _as_list(_ffn(x, mod, 1, boff, bb, tiles.ffn_rows, tiles.ffn_cols, g_ffn[1:2], *ffn_weights[1]))
    heads = (1, b, t, n_heads, HEAD_DIM)
    outs = (x, k.reshape(heads), v.reshape(heads), tail[None, :, 1:, :])
    if not emit_w:
        return outs, None
    return outs, (g_mix, g_ffn, w_qkv_bf[0], g_q, g_k, w_o_bf[0], w_pool, pool_scale, (ffn0_bf, ffn1_bf))


def kernel(x_prompt, x_sample, c_prompt, c_sample, cache_k, cache_v, state_pool, w_ada, b_ada, g_mix, g_ffn,
           w_qkv, g_q, g_k, w_o, w_pool, pool_scale, w_gate, w_up, w_down):
    bp, tp, d = x_prompt.shape
    bs, ts, _ = x_sample.shape
    past = cache_k.shape[2]
    ff = w_gate.shape[2]
    assert w_ada.shape[0] == 2 and cache_k.shape[0] == 1 and state_pool.shape[0] == 1 and w_qkv.shape[0] == 1
    assert state_pool.shape[2] == POOL_STATE and ts % HALO == 0 and d % (HEAD_DIM * len(POOL_WINDOWS)) == 0

    mod = _ada_modulation(jnp.concatenate([c_sample, c_prompt], axis=0), w_ada, b_ada)

    ffn_stacks = (w_gate, w_up, w_down)
    weights = (g_mix, g_ffn, w_qkv[0], g_q, g_k, w_o[0], w_pool[0].astype(BF16), pool_scale,
               (ffn_stacks, ffn_stacks))
    cache_shape = (bs, past, d // HEAD_DIM // SUBLANES, SUBLANES, HEAD_DIM)
    caches = (cache_k.reshape(cache_shape), cache_v.reshape(cache_shape), state_pool[0])
    sample_tiles = _Tiles(rows=ts, qkv_rows=ts, qkv_cols=min(TILE_QKV_COLS_F32, d), ffn_rows=ts,
                          ffn_cols=min(TILE_FF_F32, ff), pool_rows=ts)
    (y_s, k_s, v_s, pool_s), weights_bf = _trunk(x_sample, mod, 0, bs, sample_tiles, caches, weights)

    prompt_tiles = _Tiles(rows=min(TILE_ROWS, tp), qkv_rows=min(TILE_QKV_ROWS, tp), qkv_cols=min(TILE_QKV_COLS, d),
                          ffn_rows=min(TILE_FFN_ROWS, tp), ffn_cols=min(TILE_FF, ff),
                          pool_rows=min(TILE_POOL_ROWS, tp))
    (y_p, k_p, v_p, pool_p), _ = _trunk(x_prompt, mod, bs, 1, prompt_tiles, None,
                                        weights if weights_bf is None else weights_bf)
    return (y_p, y_s, k_p, v_p, pool_p, k_s, v_s, pool_s)
```

```python
import functools
from typing import NamedTuple

import jax
import jax.numpy as jnp
from jax import lax
from jax.experimental import pallas as pl
from jax.experimental.pallas import tpu as pltpu

F32 = jnp.float32
BF16 = jnp.bfloat16

HEAD_DIM = 128
SUBLANES = 8
EPS = 1e-6
N_MOD = 6
POOL_WINDOWS = (2, 4, 8, 16)
HALO = 16
POOL_STATE = HALO - 1

V7X_VMEM_BYTES = 64 * 1024 * 1024
VMEM_LIMIT = V7X_VMEM_BYTES - 8 * 1024 * 1024

TILE_ROWS = 1024
TILE_COLS = 1024
TILE_QKV_ROWS = 1024
TILE_QKV_COLS = 512
ROW_SPLIT = 4
TILE_QKV_COLS_F32 = 512
QKV_SUBTILE = 256
FF_DOWN_COLS = 512
TILE_FFN_ROWS = 512
TILE_FF = 512
TILE_FF_F32 = 256
FF_SUBTILE = 256
TILE_POOL_ROWS = 512
TILE_Q = 256
LOGITS_AHEAD = 2
HEADS_PER_STEP = 8
TILE_CACHE = 512
CACHE_SPLIT = 2
CACHE_HEAD_GROUP = 16
TILE_ADA = 1024


def _cparams(sem):
    return pltpu.CompilerParams(dimension_semantics=sem, vmem_limit_bytes=VMEM_LIMIT)


def _normmod(x3, g, shift, scale):
    ms = jnp.mean(x3 * x3, axis=-1, keepdims=True)
    y = x3 * lax.rsqrt(ms + EPS) * g
    return y * (1.0 + scale) + shift


def _later_matrix(n):
    j = jnp.arange(n)[:, None]
    s = jnp.arange(n)[None, :]
    u = (j >= s).astype(BF16)
    return jnp.concatenate([u, u], axis=0)


LOG2_E = 1.4426950408889634
INV_LN2 = LOG2_E
SOFTPLUS2_CLAMP = 64.0
Q_PRESCALE = HEAD_DIM ** -0.5 * LOG2_E


def _sb_blocks(z2_fn, n_chains, uu_ref, carries, mask, pv_fn):
    z2s = [z2_fn(i) for i in range(min(LOGITS_AHEAD, n_chains))]
    sums, rowsums = [], []
    for i in range(n_chains):
        if i + LOGITS_AHEAD < n_chains:
            z2s.append(z2_fn(i + LOGITS_AHEAD))
        z2 = z2s[i]
        sp = jnp.maximum(z2, jnp.log(1.0 + jnp.exp2(jnp.minimum(z2, SOFTPLUS2_CLAMP))) * INV_LN2)
        if mask is not None:
            sp = jnp.where(mask, sp, 0.0)
        hi = sp.astype(BF16)
        lo = (sp - hi.astype(F32)).astype(BF16)
        total = jnp.dot(jnp.concatenate([hi, lo], axis=1), uu_ref[...], preferred_element_type=F32)
        sums.append(total)
        rowsums.append(total[:, :1])
    outs = []
    for i, (z2, total, rs, carry) in enumerate(zip(z2s, sums, rowsums, carries)):
        arg = z2 - total
        if carry is not None:
            arg = arg + carry
        w = jnp.exp2(arg)
        if mask is not None:
            w = jnp.where(mask, w, 0.0)
        outs.append((pv_fn(i, w.astype(BF16)), rs))
    return outs


def _ada_kernel(c_ref, w_ref, b_ref, o_ref):
    c = c_ref[...]
    s = (c * jax.nn.sigmoid(c)).astype(BF16)
    o_ref[...] = jnp.dot(s, w_ref[...].astype(BF16), preferred_element_type=F32) + b_ref[...]


def _ada_modulation(c_all, w_ada, b_ada):
    n_layers, d, _ = w_ada.shape
    bt = c_all.shape[0]
    tn = min(TILE_ADA, d)
    npc = d // tn
    out = pl.pallas_call(
        _ada_kernel,
        grid=(n_layers, N_MOD * npc),
        in_specs=[
            pl.BlockSpec((bt, d), lambda l, n: (0, 0)),
            pl.BlockSpec((None, d, tn), lambda l, n: (l, 0, n)),
            pl.BlockSpec((None, 1, tn), lambda l, n: (l, 0, n)),
        ],
        out_specs=pl.BlockSpec((None, None, bt, tn), lambda l, n: (l, n // npc, 0, n % npc)),
        out_shape=jax.ShapeDtypeStruct((n_layers, N_MOD, bt, d), F32),
        compiler_params=_cparams(("arbitrary", "arbitrary")),
        name="ada_modulation",
    )(c_all, w_ada, b_ada.reshape(n_layers, 1, N_MOD * d))
    return out.reshape(n_layers, N_MOD, bt, 1, d)


def _mod_spec(layer, which, bb, boff, d, ncols=None):
    if ncols is None:
        return pl.BlockSpec((None, None, bb, 1, d), lambda b, i, *_: (layer, which, boff // bb + b, 0, 0))
    return pl.BlockSpec((None, None, bb, 1, ncols), lambda b, i, n: (layer, which, boff // bb + b, 0, n))


def _head_norm(y, g):
    outs = []
    for h in range(y.shape[1] // HEAD_DIM):
        yh = y[:, h * HEAD_DIM:(h + 1) * HEAD_DIM]
        ms = jnp.mean(yh * yh, axis=-1, keepdims=True)
        outs.append(yh * lax.rsqrt(ms + EPS) * g)
    return outs[0] if len(outs) == 1 else jnp.concatenate(outs, axis=1)


def _bf16_weight(w_ref, wb_ref):
    if wb_ref is None:
        return w_ref
    wb_ref[...] = w_ref[...].astype(BF16)
    return wb_ref


def _row_split(bb, tt):
    return ROW_SPLIT if bb == 1 and tt % (ROW_SPLIT * SUBLANES) == 0 else 1


def _row_split_specs(bb, tt, d):
    n = _row_split(bb, tt)
    return [pl.BlockSpec((bb, tt // n, d), lambda bi, i, *_, s=s: (bi, i * n + s, 0)) for s in range(n)]


def _normmod_rows(x_refs, g_ref, sh_ref, sc_ref, h_scr):
    rows = h_scr.shape[0] // len(x_refs)
    for s, x_ref in enumerate(x_refs):
        h = _normmod(x_ref[...], g_ref[...], sh_ref[...], sc_ref[...])
        h_scr[s * rows:(s + 1) * rows, :] = h.reshape(rows, h_scr.shape[1]).astype(BF16)


def _qkv_kernel(*refs, n_per, n_x):
    x_refs = refs[:n_x]
    sh_ref, sc_ref, g_ref, w_ref, gq_ref, gk_ref, q_ref, k_ref, v_ref, kb_ref, vb_ref, *rest = refs[n_x:]
    wb_ref, h_scr = rest if len(rest) == 2 else (None, *rest)
    j = pl.program_id(2)

    @pl.when(j == 0)
    def _():
        _normmod_rows(x_refs, g_ref, sh_ref, sc_ref, h_scr)

    bb, tt, tn = q_ref.shape
    sub = min(QKV_SUBTILE, tn)
    w_bf = _bf16_weight(w_ref, wb_ref)

    def project(c):
        return jnp.dot(h_scr[...], w_bf[:, c * sub:(c + 1) * sub], preferred_element_type=F32)

    def pipelined(epilogue):
        y = project(0)
        for c in range(tn // sub):
            nxt = project(c + 1) if (c + 1) * sub < tn else None
            epilogue(y, slice(c * sub, (c + 1) * sub))
            y = nxt

    def store_q(y, cols):
        q_ref[:, :, cols] = _head_norm(y, gq_ref[...] * Q_PRESCALE).astype(BF16).reshape(bb, tt, sub)

    def store_k(y, cols):
        k = _head_norm(y, gk_ref[...])
        k_ref[:, :, cols] = k.reshape(bb, tt, sub)
        kb_ref[:, :, cols] = k.astype(BF16).reshape(bb, tt, sub)

    def store_v(y, cols):
        v_ref[:, :, cols] = y.reshape(bb, tt, sub)
        vb_ref[:, :, cols] = y.astype(BF16).reshape(bb, tt, sub)

    pl.when(j < n_per)(lambda: pipelined(store_q))
    pl.when(jnp.logical_and(j >= n_per, j < 2 * n_per))(lambda: pipelined(store_k))
    pl.when(j >= 2 * n_per)(lambda: pipelined(store_v))


def _qkv_project(x, mod, boff, bb, tt, tn, g_mix, w_qkv, g_q, g_k):
    b, t, d = x.shape
    n_per = d // tn
    emit_w = w_qkv.dtype != BF16
    grid = (b // bb, t // tt, 3 * n_per)
    q_spec = pl.BlockSpec((bb, tt, tn), lambda bi, i, j: (bi, i, jnp.minimum(j, n_per - 1)))
    k_spec = pl.BlockSpec((bb, tt, tn), lambda bi, i, j: (bi, i, jnp.clip(j - n_per, 0, n_per - 1)))
    v_spec = pl.BlockSpec((bb, tt, tn), lambda bi, i, j: (bi, i, jnp.maximum(j - 2 * n_per, 0)))
    w_spec = pl.BlockSpec((d, tn), lambda bi, i, j: (0, j))
    out_specs = [q_spec, k_spec, v_spec, k_spec, v_spec]
    out_shape = [
        jax.ShapeDtypeStruct((b, t, d), BF16),
        jax.ShapeDtypeStruct((b, t, d), F32),
        jax.ShapeDtypeStruct((b, t, d), F32),
        jax.ShapeDtypeStruct((b, t, d), BF16),
        jax.ShapeDtypeStruct((b, t, d), BF16),
    ]
    if emit_w:
        assert grid[:2] == (1, 1), "each weight tile must be visited exactly once to emit its bf16 copy"
        out_specs.append(w_spec)
        out_shape.append(jax.ShapeDtypeStruct(w_qkv.shape, BF16))
    x_specs = _row_split_specs(bb, tt, d)
    return pl.pallas_call(
        functools.partial(_qkv_kernel, n_per=n_per, n_x=len(x_specs)),
        grid=grid,
        in_specs=[
            *x_specs,
            _mod_spec(0, 0, bb, boff, d),
            _mod_spec(0, 1, bb, boff, d),
            pl.BlockSpec((1, d), lambda bi, i, j: (0, 0)),
            w_spec,
            pl.BlockSpec((1, HEAD_DIM), lambda bi, i, j: (0, 0)),
            pl.BlockSpec((1, HEAD_DIM), lambda bi, i, j: (0, 0)),
        ],
        out_specs=out_specs,
        out_shape=out_shape,
        scratch_shapes=[pltpu.VMEM((bb * tt, d), BF16)],
        compiler_params=_cparams(("parallel", "parallel", "arbitrary")),
        name="qkv_project",
    )(*[x] * len(x_specs), mod, mod, g_mix, w_qkv, g_q, g_k)


def _nt_dot(a, b):
    return lax.dot_general(a, b, (((1,), (1,)), ((), ())), preferred_element_type=F32)


def _attn_prompt_kernel(q_ref, k_ref, v_ref, uu_ref, o_ref, acc_scr, carry_scr, *, tq, n_par):
    t = q_ref.shape[0]
    row = lax.broadcasted_iota(jnp.int32, (tq, tq), 0)
    col = lax.broadcasted_iota(jnp.int32, (tq, tq), 1)
    causal = col < row
    heads = [slice(h * HEAD_DIM, (h + 1) * HEAD_DIM) for h in range(n_par)]

    def blocks(qrows, krows, first):
        carries = [None if first else carry_scr[i] for i in range(n_par)]
        res = _sb_blocks(lambda i: _nt_dot(q_ref[qrows, heads[i]], k_ref[krows, heads[i]]), n_par,
                         uu_ref, carries, causal if first else None,
                         lambda i, w: jnp.dot(w, v_ref[krows, heads[i]], preferred_element_type=F32))
        for i, (pv, rs) in enumerate(res):
            if first:
                acc_scr[i] = pv
                carry_scr[i] = -rs
            else:
                acc_scr[i] += pv
                carry_scr[i] -= rs

    def query_block(qi, _):
        qrows = pl.ds(pl.multiple_of(qi * tq, tq), tq)
        blocks(qrows, qrows, True)

        def body(step, _):
            blocks(qrows, pl.ds(pl.multiple_of((qi - 1 - step) * tq, tq), tq), False)
            return 0

        lax.fori_loop(0, qi, body, 0)
        for i, hc in enumerate(heads):
            o_ref[qrows, hc] = acc_scr[i].astype(o_ref.dtype)
        return 0

    lax.fori_loop(0, t // tq, query_block, 0)


def _attn_prompt(q, k, v):
    b, t, d = q.shape
    tq = min(TILE_Q, t)
    n_par = HEADS_PER_STEP
    width = n_par * HEAD_DIM
    spec = pl.BlockSpec((None, t, width), lambda bi, h: (bi, 0, h))
    return pl.pallas_call(
        functools.partial(_attn_prompt_kernel, tq=tq, n_par=n_par),
        grid=(b, d // width),
        in_specs=[spec, spec, spec, pl.BlockSpec((2 * tq, tq), lambda bi, h: (0, 0))],
        out_specs=spec,
        out_shape=jax.ShapeDtypeStruct((b, t, d), BF16),
        scratch_shapes=[pltpu.VMEM((n_par, tq, HEAD_DIM), F32), pltpu.VMEM((n_par, tq, 1), F32)],
        compiler_params=_cparams(("parallel", "parallel")),
        name="sb_attention_prompt",
    )(q, k, v, _later_matrix(tq))


def _head_rows(group_ref, i):
    tk, group, _ = group_ref.shape
    return group_ref.reshape(tk * group, HEAD_DIM)[pl.ds(i, tk, stride=group), :]


def _attn_sample_kernel(q_ref, kn_ref, vn_ref, *refs, n_heads):
    n_groups = n_heads // CACHE_HEAD_GROUP
    n_parts = n_groups * CACHE_SPLIT
    ck_refs, cv_refs = refs[:n_parts], refs[n_parts:2 * n_parts]
    uu_ref, uun_ref, o_ref, qbd_scr, knew_scr, vnew_scr, acc_scr, carry_scr = refs[2 * n_parts:]
    j = pl.program_id(1)
    tnew, d = q_ref.shape

    def all_heads(part_refs):
        def head(h):
            g, i = divmod(h, CACHE_HEAD_GROUP)
            return jnp.concatenate([_head_rows(part_refs[s * n_groups + g], i) for s in range(CACHE_SPLIT)],
                                   axis=0)
        return jnp.concatenate([head(h) for h in range(n_heads)], axis=1).astype(BF16)

    @pl.when(j == 0)
    def _():
        q = q_ref[...]
        lane_head = lax.broadcasted_iota(jnp.int32, (tnew, d), 1) // HEAD_DIM
        for h in range(n_heads):
            qbd_scr[h * tnew:(h + 1) * tnew, :] = jnp.where(lane_head == h, q, jnp.zeros_like(q))
        knew_scr[...] = jnp.zeros_like(knew_scr)
        vnew_scr[...] = jnp.zeros_like(vnew_scr)
        knew_scr[0:tnew, :] = kn_ref[...].astype(BF16)
        vnew_scr[0:tnew, :] = vn_ref[...].astype(BF16)
        z2 = _nt_dot(qbd_scr[...], knew_scr[...])
        t_of_row = lax.broadcasted_iota(jnp.int32, z2.shape, 0) % tnew
        s_of_col = lax.broadcasted_iota(jnp.int32, z2.shape, 1)
        [(pv, rs)] = _sb_blocks(lambda i: z2, 1, uun_ref, [None], s_of_col < t_of_row,
                                lambda i, w: jnp.dot(w, vnew_scr[...], preferred_element_type=F32))
        acc_scr[...] = pv
        carry_scr[...] = -rs

    @pl.when(j > 0)
    def _():
        z2 = _nt_dot(qbd_scr[...], all_heads(ck_refs))
        vb = all_heads(cv_refs)
        [(pv, rs)] = _sb_blocks(lambda i: z2, 1, uu_ref, [carry_scr[...]], None,
                                lambda i, w: jnp.dot(w, vb, preferred_element_type=F32))
        acc_scr[...] += pv
        carry_scr[...] -= rs

    @pl.when(j == pl.num_programs(1) - 1)
    def _():
        for h in range(n_heads):
            cols = slice(h * HEAD_DIM, (h + 1) * HEAD_DIM)
            o_ref[:, cols] = acc_scr[h * tnew:(h + 1) * tnew, cols].astype(o_ref.dtype)


def _attn_sample(q, k_new, v_new, cache_k, cache_v):
    b, t, d = q.shape
    n_heads = d // HEAD_DIM
    p, n_groups = cache_k.shape[1:3]
    tk = min(TILE_CACHE, p)
    n_tiles = p // tk
    new_pad = 128
    row_spec = pl.BlockSpec((None, t, d), lambda bi, j: (bi, 0, 0))
    part = tk // CACHE_SPLIT
    cache_specs = [pl.BlockSpec((None, part, None, CACHE_HEAD_GROUP, HEAD_DIM),
                                lambda bi, j, s=s, g=g: (bi, (n_tiles - jnp.maximum(j, 1)) * CACHE_SPLIT + s, g, 0, 0))
                   for s in range(CACHE_SPLIT) for g in range(n_groups)]
    return pl.pallas_call(
        functools.partial(_attn_sample_kernel, n_heads=n_heads),
        grid=(b, n_tiles + 1),
        in_specs=[
            row_spec, row_spec, row_spec, *cache_specs, *cache_specs,
            pl.BlockSpec((2 * tk, tk), lambda bi, j: (0, 0)),
            pl.BlockSpec((2 * new_pad, new_pad), lambda bi, j: (0, 0)),
        ],
        out_specs=row_spec,
        out_shape=jax.ShapeDtypeStruct((b, t, d), BF16),
        scratch_shapes=[
            pltpu.VMEM((n_heads * t, d), BF16),
            pltpu.VMEM((new_pad, d), BF16),
            pltpu.VMEM((new_pad, d), BF16),
            pltpu.VMEM((n_heads * t, d), F32),
            pltpu.VMEM((n_heads * t, 1), F32),
        ],
        compiler_params=_cparams(("parallel", "arbitrary")),
        name="sb_attention_sample",
    )(q, k_new, v_new, *[cache_k] * len(cache_specs), *[cache_v] * len(cache_specs),
      _later_matrix(tk), _later_matrix(new_pad))


def _proj_res_kernel(a_ref, w_ref, x_ref, gate_ref, o_ref, wb_ref=None):
    bb, tt, k = a_ref.shape
    w_bf = _bf16_weight(w_ref, wb_ref)
    y = jnp.dot(a_ref[...].reshape(bb * tt, k), w_bf[...], preferred_element_type=F32)
    o_ref[...] = x_ref[...] + gate_ref[...] * y.reshape(o_ref.shape)


def _proj_residual(a, w, x, mod, layer, which, boff, bb, tt):
    b, t, k = a.shape
    d = w.shape[1]
    tn = min(TILE_COLS, d)
    grid = (b // bb, t // tt, d // tn)
    emit_w = w.dtype != BF16
    w_spec = pl.BlockSpec((k, tn), lambda bi, i, n: (0, n))
    out_specs = [pl.BlockSpec((bb, tt, tn), lambda bi, i, n: (bi, i, n))]
    out_shape = [jax.ShapeDtypeStruct((b, t, d), F32)]
    if emit_w:
        assert grid[:2] == (1, 1), "each weight tile must be visited exactly once to emit its bf16 copy"
        out_specs.append(w_spec)
        out_shape.append(jax.ShapeDtypeStruct(w.shape, BF16))
    outs = pl.pallas_call(
        _proj_res_kernel,
        grid=grid,
        in_specs=[
            pl.BlockSpec((bb, tt, k), lambda bi, i, n: (bi, i, 0)),
            w_spec,
            pl.BlockSpec((bb, tt, tn), lambda bi, i, n: (bi, i, n)),
            _mod_spec(layer, which, bb, boff, d, ncols=tn),
        ],
        out_specs=out_specs,
        out_shape=out_shape,
        compiler_params=_cparams(("parallel", "parallel", "arbitrary")),
        name="proj_residual",
    )(a, w, x, mod)
    return outs if emit_w else outs[0]


def _ffn_kernel(x_ref, sh_ref, sc_ref, gt_ref, g_ref, wg_ref, wu_ref, wd_ref, o_ref, *rest):
    wgb_ref, wub_ref, wdb_ref, h_scr = rest if len(rest) == 4 else (None, None, None, *rest)
    f = pl.program_id(2)

    @pl.when(f == 0)
    def _():
        h = _normmod(x_ref[...], g_ref[...], sh_ref[...], sc_ref[...])
        h_scr[...] = h.reshape(h_scr.shape).astype(BF16)
        o_ref[...] = jnp.zeros_like(o_ref)

    bb, tt, d = o_ref.shape
    tf = wg_ref.shape[1]
    sub = min(FF_SUBTILE, tf)
    dcols = min(FF_DOWN_COLS, d)
    wg_bf = _bf16_weight(wg_ref, wgb_ref)
    wu_bf = _bf16_weight(wu_ref, wub_ref)
    wd_bf = _bf16_weight(wd_ref, wdb_ref)

    def gate_up(c):
        cols = slice(c * sub, (c + 1) * sub)
        h = h_scr[...]
        return (jnp.dot(h, wg_bf[:, cols], preferred_element_type=F32),
                jnp.dot(h, wu_bf[:, cols], preferred_element_type=F32))

    gate, up = gate_up(0)
    for c in range(tf // sub):
        nxt = gate_up(c + 1) if (c + 1) * sub < tf else None
        a = (gate * jax.nn.sigmoid(gate) * up).astype(BF16)
        for n in range(d // dcols):
            cols = slice(n * dcols, (n + 1) * dcols)
            o_ref[:, :, cols] += jnp.dot(a, wd_bf[c * sub:(c + 1) * sub, cols],
                                         preferred_element_type=F32).reshape(bb, tt, dcols)
        if nxt is not None:
            gate, up = nxt

    @pl.when(f == pl.num_programs(2) - 1)
    def _():
        o_ref[...] = x_ref[...] + gt_ref[...] * o_ref[...]


def _layer_weight_spec(w, layer, block, index_map):
    if w.ndim == 2:
        return pl.BlockSpec(block, index_map)
    return pl.BlockSpec((None, *block), lambda *g: (layer, *index_map(*g)))


def _ffn(x, mod, layer, boff, bb, tt, tf, g_ffn, w_gate, w_up, w_down):
    b, t, d = x.shape
    ff = w_gate.shape[-1]
    grid = (b // bb, t // tt, ff // tf)
    emit_w = w_gate.dtype != BF16
    xspec = pl.BlockSpec((bb, tt, d), lambda bi, i, f: (bi, i, 0))
    up_block, up_map = (d, tf), lambda bi, i, f: (0, f)
    down_block, down_map = (tf, d), lambda bi, i, f: (f, 0)
    out_specs = [xspec]
    out_shape = [jax.ShapeDtypeStruct((b, t, d), F32)]
    if emit_w:
        assert grid[:2] == (1, 1), "each weight tile must be visited exactly once to emit its bf16 copy"
        out_specs += [pl.BlockSpec(up_block, up_map), pl.BlockSpec(up_block, up_map),
                      pl.BlockSpec(down_block, down_map)]
        out_shape += [jax.ShapeDtypeStruct((d, ff), BF16), jax.ShapeDtypeStruct((d, ff), BF16),
                      jax.ShapeDtypeStruct((ff, d), BF16)]
    outs = pl.pallas_call(
        _ffn_kernel,
        grid=grid,
        in_specs=[
            xspec,
            _mod_spec(layer, 3, bb, boff, d),
            _mod_spec(layer, 4, bb, boff, d),
            _mod_spec(layer, 5, bb, boff, d),
            pl.BlockSpec((1, d), lambda bi, i, f: (0, 0)),
            _layer_weight_spec(w_gate, layer, up_block, up_map),
            _layer_weight_spec(w_up, layer, up_block, up_map),
            _layer_weight_spec(w_down, layer, down_block, down_map),
        ],
        out_specs=out_specs,
        out_shape=out_shape,
        scratch_shapes=[pltpu.VMEM((bb * tt, d), BF16)],
        compiler_params=_cparams(("parallel", "parallel", "arbitrary")),
        name="swiglu_ffn",
    )(x, mod, mod, mod, g_ffn, w_gate, w_up, w_down)
    return outs if emit_w else outs[0]


def _pool_kernel(x_ref, halo_ref, sh_ref, sc_ref, gt_ref, g_ref, wp_ref, ps_ref, o_ref, tail_ref,
                 *, halo_is_x, n_hist):
    i = pl.program_id(1)
    bb, tt, d = x_ref.shape
    dg = d // len(POOL_WINDOWS)
    seg = HALO + tt
    x = x_ref[...]
    h = _normmod(x, g_ref[...], sh_ref[...], sc_ref[...])
    if halo_is_x:
        hist = _normmod(halo_ref[...], g_ref[...], sh_ref[...], sc_ref[...])
        hist = jnp.where(i > 0, hist, 0.0)
    else:
        hist = halo_ref[...]
    tail_ref[...] = h[:, tt - HALO:, :]

    s_all = jnp.concatenate([hist, h], axis=1).reshape(bb * seg, d)
    h2 = h.reshape(bb * tt, d)
    pos = i * tt + lax.broadcasted_iota(jnp.int32, (bb * tt, 1), 0) % tt
    outs = []
    for g, win in enumerate(POOL_WINDOWS):
        cols = slice(g * dg, (g + 1) * dg)
        s = s_all[:, cols]
        span = 1
        while span < win:
            s = s + pltpu.roll(s, span, 0)
            span *= 2
        s = s.reshape(bb, seg, dg)[:, HALO:, :].reshape(bb * tt, dg)
        count = jnp.minimum(n_hist + pos + 1, win).astype(F32)
        diff = (s / count - h2[:, cols]).astype(BF16)
        outs.append(jnp.dot(diff, wp_ref[g], preferred_element_type=F32))
    mix = (jnp.concatenate(outs, axis=1) * ps_ref[...]).reshape(bb, tt, d)
    o_ref[...] = x + gt_ref[...] * mix


def _pool_layer(x, halo, halo_is_x, n_hist, mod, layer, boff, bb, tt, g_mix, w_pool, pool_scale):
    b, t, d = x.shape
    dg = d // len(POOL_WINDOWS)
    if halo_is_x:
        per = tt // HALO
        halo_spec = pl.BlockSpec((bb, HALO, d), lambda bi, i: (bi, jnp.maximum(i * per - 1, 0), 0))
    else:
        halo_spec = pl.BlockSpec((bb, HALO, d), lambda bi, i: (bi, 0, 0))
    xspec = pl.BlockSpec((bb, tt, d), lambda bi, i: (bi, i, 0))
    return pl.pallas_call(
        functools.partial(_pool_kernel, halo_is_x=halo_is_x, n_hist=n_hist),
        grid=(b // bb, t // tt),
        in_specs=[
            xspec, halo_spec,
            _mod_spec(layer, 0, bb, boff, d),
            _mod_spec(layer, 1, bb, boff, d),
            _mod_spec(layer, 2, bb, boff, d),
            pl.BlockSpec((1, d), lambda bi, i: (0, 0)),
            pl.BlockSpec((len(POOL_WINDOWS), dg, dg), lambda bi, i: (0, 0, 0)),
            pl.BlockSpec((1, d), lambda bi, i: (0, 0)),
        ],
        out_specs=[xspec, pl.BlockSpec((bb, HALO, d), lambda bi, i: (bi, 0, 0))],
        out_shape=[jax.ShapeDtypeStruct((b, t, d), F32), jax.ShapeDtypeStruct((b, HALO, d), F32)],
        compiler_params=_cparams(("parallel", "arbitrary")),
        name="pool_mixer",
    )(x, halo, mod, mod, mod, g_mix, w_pool, pool_scale)


class _Tiles(NamedTuple):
    rows: int
    qkv_rows: int
    qkv_cols: int
    ffn_rows: int
    ffn_cols: int
    pool_rows: int


def _as_list(x):
    return list(x) if isinstance(x, (list, tuple)) else [x]


def _trunk(x, mod, boff, bb, tiles, caches, weights):
    (g_mix, g_ffn, w_qkv, g_q, g_k, w_o, w_pool, pool_scale, ffn_weights) = weights
    b, t, d = x.shape
    n_heads = d // HEAD_DIM
    emit_w = w_qkv.dtype != BF16
    q, k, v, k_bf, v_bf, *w_qkv_bf = _qkv_project(x, mod, boff, bb, tiles.qkv_rows, tiles.qkv_cols,
                                                  g_mix[0:1], w_qkv, g_q, g_k)
    if caches is None:
        o = _attn_prompt(q, k_bf, v_bf)
    else:
        o = _attn_sample(q, k, v, caches[0], caches[1])
    x, *w_o_bf = _as_list(_proj_residual(o, w_o, x, mod, 0, 2, boff, bb, tiles.rows))
    x, *ffn0_bf = _as_list(_ffn(x, mod, 0, boff, bb, tiles.ffn_rows, tiles.ffn_cols, g_ffn[0:1], *ffn_weights[0]))
    if caches is None:
        x, tail = _pool_layer(x, x, True, 0, mod, 1, boff, bb, tiles.pool_rows, g_mix[1:2], w_pool, pool_scale)
    else:
        hist = jnp.pad(caches[2], ((0, 0), (1, 0), (0, 0)))
        x, tail = _pool_layer(x, hist, False, caches[0].shape[1], mod, 1, boff, bb, tiles.pool_rows,
                              g_mix[1:2], w_pool, pool_scale)
    x, *ffn1_bf = _as_list(_ffn(x, mod, 1, boff, bb, tiles.ffn_rows, tiles.ffn_cols, g_ffn[1:2], *ffn_weights[1]))
    heads = (1, b, t, n_heads, HEAD_DIM)
    outs = (x, k.reshape(heads), v.reshape(heads), tail[None, :, 1:, :])
    if not emit_w:
        return outs, None
    return outs, (g_mix, g_ffn, w_qkv_bf[0], g_q, g_k, w_o_bf[0], w_pool, pool_scale, (ffn0_bf, ffn1_bf))


def kernel(x_prompt, x_sample, c_prompt, c_sample, cache_k, cache_v, state_pool, w_ada, b_ada, g_mix, g_ffn,
           w_qkv, g_q, g_k, w_o, w_pool, pool_scale, w_gate, w_up, w_down):
    bp, tp, d = x_prompt.shape
    bs, ts, _ = x_sample.shape
    past = cache_k.shape[2]
    ff = w_gate.shape[2]
    assert w_ada.shape[0] == 2 and cache_k.shape[0] == 1 and state_pool.shape[0] == 1 and w_qkv.shape[0] == 1
    assert state_pool.shape[2] == POOL_STATE and ts % HALO == 0 and d % (HEAD_DIM * len(POOL_WINDOWS)) == 0

    mod = _ada_modulation(jnp.concatenate([c_sample, c_prompt], axis=0), w_ada, b_ada)

    ffn_stacks = (w_gate, w_up, w_down)
    weights = (g_mix, g_ffn, w_qkv[0], g_q, g_k, w_o[0], w_pool[0].astype(BF16), pool_scale,
               (ffn_stacks, ffn_stacks))
    assert CACHE_HEAD_GROUP % SUBLANES == 0 and (d // HEAD_DIM) % CACHE_HEAD_GROUP == 0
    cache_shape = (bs, past, d // HEAD_DIM // CACHE_HEAD_GROUP, CACHE_HEAD_GROUP, HEAD_DIM)
    caches = (cache_k.reshape(cache_shape), cache_v.reshape(cache_shape), state_pool[0])
    sample_tiles = _Tiles(rows=ts, qkv_rows=ts, qkv_cols=min(TILE_QKV_COLS_F32, d), ffn_rows=ts,
                          ffn_cols=min(TILE_FF_F32, ff), pool_rows=ts)
    (y_s, k_s, v_s, pool_s), weights_bf = _trunk(x_sample, mod, 0, bs, sample_tiles, caches, weights)

    prompt_tiles = _Tiles(rows=min(TILE_ROWS, tp), qkv_rows=min(TILE_QKV_ROWS, tp), qkv_cols=min(TILE_QKV_COLS, d),
                          ffn_rows=min(TILE_FFN_ROWS, tp), ffn_cols=min(TILE_FF, ff),
                          pool_rows=min(TILE_POOL_ROWS, tp))
    (y_p, k_p, v_p, pool_p), _ = _trunk(x_prompt, mod, bs, 1, prompt_tiles, None,
                                        weights if weights_bf is None else weights_bf)
    return (y_p, y_s, k_p, v_p, pool_p, k_s, v_s, pool_s)
```

```python
import functools
from typing import NamedTuple

import jax
import jax.numpy as jnp
from jax import lax
from jax.experimental import pallas as pl
from jax.experimental.pallas import tpu as pltpu

F32 = jnp.float32
BF16 = jnp.bfloat16

HEAD_DIM = 128
SUBLANES = 8
EPS = 1e-6
N_MOD = 6
POOL_WINDOWS = (2, 4, 8, 16)
HALO = 16
POOL_STATE = HALO - 1

V7X_VMEM_BYTES = 64 * 1024 * 1024
VMEM_LIMIT = V7X_VMEM_BYTES - 8 * 1024 * 1024

TILE_ROWS = 1024
TILE_COLS = 1024
TILE_QKV_ROWS = 1024
TILE_QKV_COLS = 512
ROW_SPLIT = 4
TILE_QKV_COLS_F32 = 512
QKV_SUBTILE = 256
FF_DOWN_COLS = 512
TILE_FFN_ROWS = 512
TILE_FF = 512
TILE_FF_F32 = 256
FF_SUBTILE = 256
TILE_POOL_ROWS = 512
TILE_Q = 256
LOGITS_AHEAD = 2
HEADS_PER_STEP = 8
TILE_CACHE = 512
TILE_ADA = 1024


def _cparams(sem):
    return pltpu.CompilerParams(dimension_semantics=sem, vmem_limit_bytes=VMEM_LIMIT)


def _normmod(x3, g, shift, scale):
    ms = jnp.mean(x3 * x3, axis=-1, keepdims=True)
    y = x3 * lax.rsqrt(ms + EPS) * g
    return y * (1.0 + scale) + shift


def _later_matrix(n):
    j = jnp.arange(n)[:, None]
    s = jnp.arange(n)[None, :]
    u = (j >= s).astype(BF16)
    return jnp.concatenate([u, u], axis=0)


LOG2_E = 1.4426950408889634
INV_LN2 = LOG2_E
SOFTPLUS2_CLAMP = 64.0
Q_PRESCALE = HEAD_DIM ** -0.5 * LOG2_E


def _sb_blocks(z2_fn, n_chains, uu_ref, carries, mask, pv_fn):
    z2s = [z2_fn(i) for i in range(min(LOGITS_AHEAD, n_chains))]
    sums, rowsums = [], []
    for i in range(n_chains):
        if i + LOGITS_AHEAD < n_chains:
            z2s.append(z2_fn(i + LOGITS_AHEAD))
        z2 = z2s[i]
        sp = jnp.maximum(z2, jnp.log(1.0 + jnp.exp2(jnp.minimum(z2, SOFTPLUS2_CLAMP))) * INV_LN2)
        if mask is not None:
            sp = jnp.where(mask, sp, 0.0)
        hi = sp.astype(BF16)
        lo = (sp - hi.astype(F32)).astype(BF16)
        total = jnp.dot(jnp.concatenate([hi, lo], axis=1), uu_ref[...], preferred_element_type=F32)
        sums.append(total)
        rowsums.append(total[:, :1])
    outs = []
    for i, (z2, total, rs, carry) in enumerate(zip(z2s, sums, rowsums, carries)):
        arg = z2 - total
        if carry is not None:
            arg = arg + carry
        w = jnp.exp2(arg)
        if mask is not None:
            w = jnp.where(mask, w, 0.0)
        outs.append((pv_fn(i, w.astype(BF16)), rs))
    return outs


def _ada_kernel(c_ref, w_ref, b_ref, o_ref):
    c = c_ref[...]
    s = (c * jax.nn.sigmoid(c)).astype(BF16)
    o_ref[...] = jnp.dot(s, w_ref[...].astype(BF16), preferred_element_type=F32) + b_ref[...]


def _ada_modulation(c_all, w_ada, b_ada):
    n_layers, d, _ = w_ada.shape
    bt = c_all.shape[0]
    tn = min(TILE_ADA, d)
    npc = d // tn
    out = pl.pallas_call(
        _ada_kernel,
        grid=(n_layers, N_MOD * npc),
        in_specs=[
            pl.BlockSpec((bt, d), lambda l, n: (0, 0)),
            pl.BlockSpec((None, d, tn), lambda l, n: (l, 0, n)),
            pl.BlockSpec((None, 1, tn), lambda l, n: (l, 0, n)),
        ],
        out_specs=pl.BlockSpec((None, None, bt, tn), lambda l, n: (l, n // npc, 0, n % npc)),
        out_shape=jax.ShapeDtypeStruct((n_layers, N_MOD, bt, d), F32),
        compiler_params=_cparams(("arbitrary", "arbitrary")),
        name="ada_modulation",
    )(c_all, w_ada, b_ada.reshape(n_layers, 1, N_MOD * d))
    return out.reshape(n_layers, N_MOD, bt, 1, d)


def _mod_spec(layer, which, bb, boff, d, ncols=None):
    if ncols is None:
        return pl.BlockSpec((None, None, bb, 1, d), lambda b, i, *_: (layer, which, boff // bb + b, 0, 0))
    return pl.BlockSpec((None, None, bb, 1, ncols), lambda b, i, n: (layer, which, boff // bb + b, 0, n))


def _head_ones(n):
    head = jnp.arange(n) // HEAD_DIM
    return (head[:, None] == head[None, :]).astype(BF16)


def _head_norm(y, g, ones_ref):
    ss = jnp.dot((y * y).astype(BF16), ones_ref[...], preferred_element_type=F32)
    gain = jnp.concatenate([g] * (y.shape[1] // HEAD_DIM), axis=1)
    return y * lax.rsqrt(ss * (1.0 / HEAD_DIM) + EPS) * gain


def _bf16_weight(w_ref, wb_ref):
    if wb_ref is None:
        return w_ref
    wb_ref[...] = w_ref[...].astype(BF16)
    return wb_ref


def _row_split(bb, tt):
    return ROW_SPLIT if bb == 1 and tt % (ROW_SPLIT * SUBLANES) == 0 else 1


def _row_split_specs(bb, tt, d):
    n = _row_split(bb, tt)
    return [pl.BlockSpec((bb, tt // n, d), lambda bi, i, *_, s=s: (bi, i * n + s, 0)) for s in range(n)]


def _normmod_rows(x_refs, g_ref, sh_ref, sc_ref, h_scr):
    rows = h_scr.shape[0] // len(x_refs)
    for s, x_ref in enumerate(x_refs):
        h = _normmod(x_ref[...], g_ref[...], sh_ref[...], sc_ref[...])
        h_scr[s * rows:(s + 1) * rows, :] = h.reshape(rows, h_scr.shape[1]).astype(BF16)


def _qkv_kernel(*refs, n_per, n_x):
    x_refs = refs[:n_x]
    (sh_ref, sc_ref, g_ref, w_ref, gq_ref, gk_ref, ones_ref,
     q_ref, k_ref, v_ref, kb_ref, vb_ref, *rest) = refs[n_x:]
    wb_ref, h_scr = rest if len(rest) == 2 else (None, *rest)
    j = pl.program_id(2)

    @pl.when(j == 0)
    def _():
        _normmod_rows(x_refs, g_ref, sh_ref, sc_ref, h_scr)

    bb, tt, tn = q_ref.shape
    sub = min(QKV_SUBTILE, tn)
    w_bf = _bf16_weight(w_ref, wb_ref)

    def project(c):
        return jnp.dot(h_scr[...], w_bf[:, c * sub:(c + 1) * sub], preferred_element_type=F32)

    def pipelined(epilogue):
        y = project(0)
        for c in range(tn // sub):
            nxt = project(c + 1) if (c + 1) * sub < tn else None
            epilogue(y, slice(c * sub, (c + 1) * sub))
            y = nxt

    def store_q(y, cols):
        q_ref[:, :, cols] = _head_norm(y, gq_ref[...] * Q_PRESCALE, ones_ref).astype(BF16).reshape(bb, tt, sub)

    def store_k(y, cols):
        k = _head_norm(y, gk_ref[...], ones_ref)
        k_ref[:, :, cols] = k.reshape(bb, tt, sub)
        kb_ref[:, :, cols] = k.astype(BF16).reshape(bb, tt, sub)

    def store_v(y, cols):
        v_ref[:, :, cols] = y.reshape(bb, tt, sub)
        vb_ref[:, :, cols] = y.astype(BF16).reshape(bb, tt, sub)

    pl.when(j < n_per)(lambda: pipelined(store_q))
    pl.when(jnp.logical_and(j >= n_per, j < 2 * n_per))(lambda: pipelined(store_k))
    pl.when(j >= 2 * n_per)(lambda: pipelined(store_v))


def _qkv_project(x, mod, boff, bb, tt, tn, g_mix, w_qkv, g_q, g_k):
    b, t, d = x.shape
    n_per = d // tn
    sub = min(QKV_SUBTILE, tn)
    emit_w = w_qkv.dtype != BF16
    grid = (b // bb, t // tt, 3 * n_per)
    q_spec = pl.BlockSpec((bb, tt, tn), lambda bi, i, j: (bi, i, jnp.minimum(j, n_per - 1)))
    k_spec = pl.BlockSpec((bb, tt, tn), lambda bi, i, j: (bi, i, jnp.clip(j - n_per, 0, n_per - 1)))
    v_spec = pl.BlockSpec((bb, tt, tn), lambda bi, i, j: (bi, i, jnp.maximum(j - 2 * n_per, 0)))
    w_spec = pl.BlockSpec((d, tn), lambda bi, i, j: (0, j))
    out_specs = [q_spec, k_spec, v_spec, k_spec, v_spec]
    out_shape = [
        jax.ShapeDtypeStruct((b, t, d), BF16),
        jax.ShapeDtypeStruct((b, t, d), F32),
        jax.ShapeDtypeStruct((b, t, d), F32),
        jax.ShapeDtypeStruct((b, t, d), BF16),
        jax.ShapeDtypeStruct((b, t, d), BF16),
    ]
    if emit_w:
        assert grid[:2] == (1, 1), "each weight tile must be visited exactly once to emit its bf16 copy"
        out_specs.append(w_spec)
        out_shape.append(jax.ShapeDtypeStruct(w_qkv.shape, BF16))
    x_specs = _row_split_specs(bb, tt, d)
    return pl.pallas_call(
        functools.partial(_qkv_kernel, n_per=n_per, n_x=len(x_specs)),
        grid=grid,
        in_specs=[
            *x_specs,
            _mod_spec(0, 0, bb, boff, d),
            _mod_spec(0, 1, bb, boff, d),
            pl.BlockSpec((1, d), lambda bi, i, j: (0, 0)),
            w_spec,
            pl.BlockSpec((1, HEAD_DIM), lambda bi, i, j: (0, 0)),
            pl.BlockSpec((1, HEAD_DIM), lambda bi, i, j: (0, 0)),
            pl.BlockSpec((sub, sub), lambda bi, i, j: (0, 0)),
        ],
        out_specs=out_specs,
        out_shape=out_shape,
        scratch_shapes=[pltpu.VMEM((bb * tt, d), BF16)],
        compiler_params=_cparams(("parallel", "parallel", "arbitrary")),
        name="qkv_project",
    )(*[x] * len(x_specs), mod, mod, g_mix, w_qkv, g_q, g_k, _head_ones(sub))


def _nt_dot(a, b):
    return lax.dot_general(a, b, (((1,), (1,)), ((), ())), preferred_element_type=F32)


def _attn_prompt_kernel(q_ref, k_ref, v_ref, uu_ref, o_ref, acc_scr, carry_scr, *, tq, n_par):
    t = q_ref.shape[0]
    row = lax.broadcasted_iota(jnp.int32, (tq, tq), 0)
    col = lax.broadcasted_iota(jnp.int32, (tq, tq), 1)
    causal = col < row
    heads = [slice(h * HEAD_DIM, (h + 1) * HEAD_DIM) for h in range(n_par)]

    def blocks(qrows, krows, first):
        carries = [None if first else carry_scr[i] for i in range(n_par)]
        res = _sb_blocks(lambda i: _nt_dot(q_ref[qrows, heads[i]], k_ref[krows, heads[i]]), n_par,
                         uu_ref, carries, causal if first else None,
                         lambda i, w: jnp.dot(w, v_ref[krows, heads[i]], preferred_element_type=F32))
        for i, (pv, rs) in enumerate(res):
            if first:
                acc_scr[i] = pv
                carry_scr[i] = -rs
            else:
                acc_scr[i] += pv
                carry_scr[i] -= rs

    def query_block(qi, _):
        qrows = pl.ds(pl.multiple_of(qi * tq, tq), tq)
        blocks(qrows, qrows, True)

        def body(step, _):
            blocks(qrows, pl.ds(pl.multiple_of((qi - 1 - step) * tq, tq), tq), False)
            return 0

        lax.fori_loop(0, qi, body, 0)
        for i, hc in enumerate(heads):
            o_ref[qrows, hc] = acc_scr[i].astype(o_ref.dtype)
        return 0

    lax.fori_loop(0, t // tq, query_block, 0)


def _attn_prompt(q, k, v):
    b, t, d = q.shape
    tq = min(TILE_Q, t)
    n_par = HEADS_PER_STEP
    width = n_par * HEAD_DIM
    spec = pl.BlockSpec((None, t, width), lambda bi, h: (bi, 0, h))
    return pl.pallas_call(
        functools.partial(_attn_prompt_kernel, tq=tq, n_par=n_par),
        grid=(b, d // width),
        in_specs=[spec, spec, spec, pl.BlockSpec((2 * tq, tq), lambda bi, h: (0, 0))],
        out_specs=spec,
        out_shape=jax.ShapeDtypeStruct((b, t, d), BF16),
        scratch_shapes=[pltpu.VMEM((n_par, tq, HEAD_DIM), F32), pltpu.VMEM((n_par, tq, 1), F32)],
        compiler_params=_cparams(("parallel", "parallel")),
        name="sb_attention_prompt",
    )(q, k, v, _later_matrix(tq))


def _attn_sample_kernel(q_ref, kn_ref, vn_ref, ck_ref, cv_ref, uu_ref, uun_ref, o_ref,
                        qbd_scr, knew_scr, vnew_scr, acc_scr, carry_scr, *, n_heads):
    j = pl.program_id(1)
    tnew, d = q_ref.shape
    tk = ck_ref.shape[0] // n_heads

    def all_heads(ref):
        return jnp.concatenate([ref[pl.ds(h, tk, stride=n_heads), :] for h in range(n_heads)],
                               axis=1).astype(BF16)

    @pl.when(j == 0)
    def _():
        q = q_ref[...]
        lane_head = lax.broadcasted_iota(jnp.int32, (tnew, d), 1) // HEAD_DIM
        for h in range(n_heads):
            qbd_scr[h * tnew:(h + 1) * tnew, :] = jnp.where(lane_head == h, q, jnp.zeros_like(q))
        knew_scr[...] = jnp.zeros_like(knew_scr)
        vnew_scr[...] = jnp.zeros_like(vnew_scr)
        knew_scr[0:tnew, :] = kn_ref[...].astype(BF16)
        vnew_scr[0:tnew, :] = vn_ref[...].astype(BF16)
        z2 = _nt_dot(qbd_scr[...], knew_scr[...])
        t_of_row = lax.broadcasted_iota(jnp.int32, z2.shape, 0) % tnew
        s_of_col = lax.broadcasted_iota(jnp.int32, z2.shape, 1)
        [(pv, rs)] = _sb_blocks(lambda i: z2, 1, uun_ref, [None], s_of_col < t_of_row,
                                lambda i, w: jnp.dot(w, vnew_scr[...], preferred_element_type=F32))
        acc_scr[...] = pv
        carry_scr[...] = -rs

    @pl.when(j > 0)
    def _():
        z2 = _nt_dot(qbd_scr[...], all_heads(ck_ref))
        vb = all_heads(cv_ref)
        [(pv, rs)] = _sb_blocks(lambda i: z2, 1, uu_ref, [carry_scr[...]], None,
                                lambda i, w: jnp.dot(w, vb, preferred_element_type=F32))
        acc_scr[...] += pv
        carry_scr[...] -= rs

    @pl.when(j == pl.num_programs(1) - 1)
    def _():
        for h in range(n_heads):
            cols = slice(h * HEAD_DIM, (h + 1) * HEAD_DIM)
            o_ref[:, cols] = acc_scr[h * tnew:(h + 1) * tnew, cols].astype(o_ref.dtype)


def _attn_sample(q, k_new, v_new, cache_k, cache_v):
    b, t, d = q.shape
    n_heads = d // HEAD_DIM
    p = cache_k.shape[1] // n_heads
    tk = min(TILE_CACHE, p)
    n_tiles = p // tk
    new_pad = 128
    row_spec = pl.BlockSpec((None, t, d), lambda bi, j: (bi, 0, 0))
    cache_spec = pl.BlockSpec((None, tk * n_heads, HEAD_DIM),
                              lambda bi, j: (bi, n_tiles - jnp.maximum(j, 1), 0))
    return pl.pallas_call(
        functools.partial(_attn_sample_kernel, n_heads=n_heads),
        grid=(b, n_tiles + 1),
        in_specs=[
            row_spec, row_spec, row_spec, cache_spec, cache_spec,
            pl.BlockSpec((2 * tk, tk), lambda bi, j: (0, 0)),
            pl.BlockSpec((2 * new_pad, new_pad), lambda bi, j: (0, 0)),
        ],
        out_specs=row_spec,
        out_shape=jax.ShapeDtypeStruct((b, t, d), BF16),
        scratch_shapes=[
            pltpu.VMEM((n_heads * t, d), BF16),
            pltpu.VMEM((new_pad, d), BF16),
            pltpu.VMEM((new_pad, d), BF16),
            pltpu.VMEM((n_heads * t, d), F32),
            pltpu.VMEM((n_heads * t, 1), F32),
        ],
        compiler_params=_cparams(("parallel", "arbitrary")),
        name="sb_attention_sample",
    )(q, k_new, v_new, cache_k, cache_v, _later_matrix(tk), _later_matrix(new_pad))


def _proj_res_kernel(a_ref, w_ref, x_ref, gate_ref, o_ref, wb_ref=None):
    bb, tt, k = a_ref.shape
    w_bf = _bf16_weight(w_ref, wb_ref)
    y = jnp.dot(a_ref[...].reshape(bb * tt, k), w_bf[...], preferred_element_type=F32)
    o_ref[...] = x_ref[...] + gate_ref[...] * y.reshape(o_ref.shape)


def _proj_residual(a, w, x, mod, layer, which, boff, bb, tt):
    b, t, k = a.shape
    d = w.shape[1]
    tn = min(TILE_COLS, d)
    grid = (b // bb, t // tt, d // tn)
    emit_w = w.dtype != BF16
    w_spec = pl.BlockSpec((k, tn), lambda bi, i, n: (0, n))
    out_specs = [pl.BlockSpec((bb, tt, tn), lambda bi, i, n: (bi, i, n))]
    out_shape = [jax.ShapeDtypeStruct((b, t, d), F32)]
    if emit_w:
        assert grid[:2] == (1, 1), "each weight tile must be visited exactly once to emit its bf16 copy"
        out_specs.append(w_spec)
        out_shape.append(jax.ShapeDtypeStruct(w.shape, BF16))
    outs = pl.pallas_call(
        _proj_res_kernel,
        grid=grid,
        in_specs=[
            pl.BlockSpec((bb, tt, k), lambda bi, i, n: (bi, i, 0)),
            w_spec,
            pl.BlockSpec((bb, tt, tn), lambda bi, i, n: (bi, i, n)),
            _mod_spec(layer, which, bb, boff, d, ncols=tn),
        ],
        out_specs=out_specs,
        out_shape=out_shape,
        compiler_params=_cparams(("parallel", "parallel", "arbitrary")),
        name="proj_residual",
    )(a, w, x, mod)
    return outs if emit_w else outs[0]


def _ffn_kernel(x_ref, sh_ref, sc_ref, gt_ref, g_ref, wg_ref, wu_ref, wd_ref, o_ref, *rest):
    wgb_ref, wub_ref, wdb_ref, h_scr = rest if len(rest) == 4 else (None, None, None, *rest)
    f = pl.program_id(2)

    @pl.when(f == 0)
    def _():
        h = _normmod(x_ref[...], g_ref[...], sh_ref[...], sc_ref[...])
        h_scr[...] = h.reshape(h_scr.shape).astype(BF16)
        o_ref[...] = jnp.zeros_like(o_ref)

    bb, tt, d = o_ref.shape
    tf = wg_ref.shape[1]
    sub = min(FF_SUBTILE, tf)
    dcols = min(FF_DOWN_COLS, d)
    wg_bf = _bf16_weight(wg_ref, wgb_ref)
    wu_bf = _bf16_weight(wu_ref, wub_ref)
    wd_bf = _bf16_weight(wd_ref, wdb_ref)

    def gate_up(c):
        cols = slice(c * sub, (c + 1) * sub)
        h = h_scr[...]
        return (jnp.dot(h, wg_bf[:, cols], preferred_element_type=F32),
                jnp.dot(h, wu_bf[:, cols], preferred_element_type=F32))

    gate, up = gate_up(0)
    for c in range(tf // sub):
        nxt = gate_up(c + 1) if (c + 1) * sub < tf else None
        a = (gate * jax.nn.sigmoid(gate) * up).astype(BF16)
        for n in range(d // dcols):
            cols = slice(n * dcols, (n + 1) * dcols)
            o_ref[:, :, cols] += jnp.dot(a, wd_bf[c * sub:(c + 1) * sub, cols],
                                         preferred_element_type=F32).reshape(bb, tt, dcols)
        if nxt is not None:
            gate, up = nxt

    @pl.when(f == pl.num_programs(2) - 1)
    def _():
        o_ref[...] = x_ref[...] + gt_ref[...] * o_ref[...]


def _layer_weight_spec(w, layer, block, index_map):
    if w.ndim == 2:
        return pl.BlockSpec(block, index_map)
    return pl.BlockSpec((None, *block), lambda *g: (layer, *index_map(*g)))


def _ffn(x, mod, layer, boff, bb, tt, tf, g_ffn, w_gate, w_up, w_down):
    b, t, d = x.shape
    ff = w_gate.shape[-1]
    grid = (b // bb, t // tt, ff // tf)
    emit_w = w_gate.dtype != BF16
    xspec = pl.BlockSpec((bb, tt, d), lambda bi, i, f: (bi, i, 0))
    up_block, up_map = (d, tf), lambda bi, i, f: (0, f)
    down_block, down_map = (tf, d), lambda bi, i, f: (f, 0)
    out_specs = [xspec]
    out_shape = [jax.ShapeDtypeStruct((b, t, d), F32)]
    if emit_w:
        assert grid[:2] == (1, 1), "each weight tile must be visited exactly once to emit its bf16 copy"
        out_specs += [pl.BlockSpec(up_block, up_map), pl.BlockSpec(up_block, up_map),
                      pl.BlockSpec(down_block, down_map)]
        out_shape += [jax.ShapeDtypeStruct((d, ff), BF16), jax.ShapeDtypeStruct((d, ff), BF16),
                      jax.ShapeDtypeStruct((ff, d), BF16)]
    outs = pl.pallas_call(
        _ffn_kernel,
        grid=grid,
        in_specs=[
            xspec,
            _mod_spec(layer, 3, bb, boff, d),
            _mod_spec(layer, 4, bb, boff, d),
            _mod_spec(layer, 5, bb, boff, d),
            pl.BlockSpec((1, d), lambda bi, i, f: (0, 0)),
            _layer_weight_spec(w_gate, layer, up_block, up_map),
            _layer_weight_spec(w_up, layer, up_block, up_map),
            _layer_weight_spec(w_down, layer, down_block, down_map),
        ],
        out_specs=out_specs,
        out_shape=out_shape,
        scratch_shapes=[pltpu.VMEM((bb * tt, d), BF16)],
        compiler_params=_cparams(("parallel", "parallel", "arbitrary")),
        name="swiglu_ffn",
    )(x, mod, mod, mod, g_ffn, w_gate, w_up, w_down)
    return outs if emit_w else outs[0]


def _pool_kernel(x_ref, halo_ref, sh_ref, sc_ref, gt_ref, g_ref, wp_ref, ps_ref, o_ref, tail_ref,
                 *, halo_is_x, n_hist):
    i = pl.program_id(1)
    bb, tt, d = x_ref.shape
    dg = d // len(POOL_WINDOWS)
    seg = HALO + tt
    x = x_ref[...]
    h = _normmod(x, g_ref[...], sh_ref[...], sc_ref[...])
    if halo_is_x:
        hist = _normmod(halo_ref[...], g_ref[...], sh_ref[...], sc_ref[...])
        hist = jnp.where(i > 0, hist, 0.0)
    else:
        hist = halo_ref[...]
    tail_ref[...] = h[:, tt - HALO:, :]

    s_all = jnp.concatenate([hist, h], axis=1).reshape(bb * seg, d)
    h2 = h.reshape(bb * tt, d)
    pos = i * tt + lax.broadcasted_iota(jnp.int32, (bb * tt, 1), 0) % tt
    outs = []
    for g, win in enumerate(POOL_WINDOWS):
        cols = slice(g * dg, (g + 1) * dg)
        s = s_all[:, cols]
        span = 1
        while span < win:
            s = s + pltpu.roll(s, span, 0)
            span *= 2
        s = s.reshape(bb, seg, dg)[:, HALO:, :].reshape(bb * tt, dg)
        count = jnp.minimum(n_hist + pos + 1, win).astype(F32)
        diff = (s / count - h2[:, cols]).astype(BF16)
        outs.append(jnp.dot(diff, wp_ref[g], preferred_element_type=F32))
    mix = (jnp.concatenate(outs, axis=1) * ps_ref[...]).reshape(bb, tt, d)
    o_ref[...] = x + gt_ref[...] * mix


def _pool_layer(x, halo, halo_is_x, n_hist, mod, layer, boff, bb, tt, g_mix, w_pool, pool_scale):
    b, t, d = x.shape
    dg = d // len(POOL_WINDOWS)
    if halo_is_x:
        per = tt // HALO
        halo_spec = pl.BlockSpec((bb, HALO, d), lambda bi, i: (bi, jnp.maximum(i * per - 1, 0), 0))
    else:
        halo_spec = pl.BlockSpec((bb, HALO, d), lambda bi, i: (bi, 0, 0))
    xspec = pl.BlockSpec((bb, tt, d), lambda bi, i: (bi, i, 0))
    return pl.pallas_call(
        functools.partial(_pool_kernel, halo_is_x=halo_is_x, n_hist=n_hist),
        grid=(b // bb, t // tt),
        in_specs=[
            xspec, halo_spec,
            _mod_spec(layer, 0, bb, boff, d),
            _mod_spec(layer, 1, bb, boff, d),
            _mod_spec(layer, 2, bb, boff, d),
            pl.BlockSpec((1, d), lambda bi, i: (0, 0)),
            pl.BlockSpec((len(POOL_WINDOWS), dg, dg), lambda bi, i: (0, 0, 0)),
            pl.BlockSpec((1, d), lambda bi, i: (0, 0)),
        ],
        out_specs=[xspec, pl.BlockSpec((bb, HALO, d), lambda bi, i: (bi, 0, 0))],
        out_shape=[jax.ShapeDtypeStruct((b, t, d), F32), jax.ShapeDtypeStruct((b, HALO, d), F32)],
        compiler_params=_cparams(("parallel", "arbitrary")),
        name="pool_mixer",
    )(x, halo, mod, mod, mod, g_mix, w_pool, pool_scale)


class _Tiles(NamedTuple):
    rows: int
    qkv_rows: int
    qkv_cols: int
    ffn_rows: int
    ffn_cols: int
    pool_rows: int


def _as_list(x):
    return list(x) if isinstance(x, (list, tuple)) else [x]


def _trunk(x, mod, boff, bb, tiles, caches, weights):
    (g_mix, g_ffn, w_qkv, g_q, g_k, w_o, w_pool, pool_scale, ffn_weights) = weights
    b, t, d = x.shape
    n_heads = d // HEAD_DIM
    emit_w = w_qkv.dtype != BF16
    q, k, v, k_bf, v_bf, *w_qkv_bf = _qkv_project(x, mod, boff, bb, tiles.qkv_rows, tiles.qkv_cols,
                                                  g_mix[0:1], w_qkv, g_q, g_k)
    if caches is None:
        o = _attn_prompt(q, k_bf, v_bf)
    else:
        o = _attn_sample(q, k, v, caches[0], caches[1])
    x, *w_o_bf = _as_list(_proj_residual(o, w_o, x, mod, 0, 2, boff, bb, tiles.rows))
    x, *ffn0_bf = _as_list(_ffn(x, mod, 0, boff, bb, tiles.ffn_rows, tiles.ffn_cols, g_ffn[0:1], *ffn_weights[0]))
    if caches is None:
        x, tail = _pool_layer(x, x, True, 0, mod, 1, boff, bb, tiles.pool_rows, g_mix[1:2], w_pool, pool_scale)
    else:
        hist = jnp.pad(caches[2], ((0, 0), (1, 0), (0, 0)))
        x, tail = _pool_layer(x, hist, False, caches[0].shape[1] // n_heads, mod, 1, boff, bb, tiles.pool_rows,
                              g_mix[1:2], w_pool, pool_scale)
    x, *ffn1_bf = _as_list(_ffn(x, mod, 1, boff, bb, tiles.ffn_rows, tiles.ffn_cols, g_ffn[1:2], *ffn_weights[1]))
    heads = (1, b, t, n_heads, HEAD_DIM)
    outs = (x, k.reshape(heads), v.reshape(heads), tail[None, :, 1:, :])
    if not emit_w:
        return outs, None
    return outs, (g_mix, g_ffn, w_qkv_bf[0], g_q, g_k, w_o_bf[0], w_pool, pool_scale, (ffn0_bf, ffn1_bf))


def kernel(x_prompt, x_sample, c_prompt, c_sample, cache_k, cache_v, state_pool, w_ada, b_ada, g_mix, g_ffn,
           w_qkv, g_q, g_k, w_o, w_pool, pool_scale, w_gate, w_up, w_down):
    bp, tp, d = x_prompt.shape
    bs, ts, _ = x_sample.shape
    past = cache_k.shape[2]
    ff = w_gate.shape[2]
    assert w_ada.shape[0] == 2 and cache_k.shape[0] == 1 and state_pool.shape[0] == 1 and w_qkv.shape[0] == 1
    assert state_pool.shape[2] == POOL_STATE and ts % HALO == 0 and d % (HEAD_DIM * len(POOL_WINDOWS)) == 0

    mod = _ada_modulation(jnp.concatenate([c_sample, c_prompt], axis=0), w_ada, b_ada)

    ffn_stacks = (w_gate, w_up, w_down)
    weights = (g_mix, g_ffn, w_qkv[0], g_q, g_k, w_o[0], w_pool[0].astype(BF16), pool_scale,
               (ffn_stacks, ffn_stacks))
    n_heads = d // HEAD_DIM
    assert n_heads % SUBLANES == 0
    caches = (cache_k.reshape(bs, past * n_heads, HEAD_DIM), cache_v.reshape(bs, past * n_heads, HEAD_DIM),
              state_pool[0])
    sample_tiles = _Tiles(rows=ts, qkv_rows=ts, qkv_cols=min(TILE_QKV_COLS_F32, d), ffn_rows=ts,
                          ffn_cols=min(TILE_FF_F32, ff), pool_rows=ts)
    (y_s, k_s, v_s, pool_s), weights_bf = _trunk(x_sample, mod, 0, bs, sample_tiles, caches, weights)

    prompt_tiles = _Tiles(rows=min(TILE_ROWS, tp), qkv_rows=min(TILE_QKV_ROWS, tp), qkv_cols=min(TILE_QKV_COLS, d),
                          ffn_rows=min(TILE_FFN_ROWS, tp), ffn_cols=min(TILE_FF, ff),
                          pool_rows=min(TILE_POOL_ROWS, tp))
    (y_p, k_p, v_p, pool_p), _ = _trunk(x_prompt, mod, bs, 1, prompt_tiles, None,
                                        weights if weights_bf is None else weights_bf)
    return (y_p, y_s, k_p, v_p, pool_p, k_s, v_s, pool_s)
```

```python
import functools
from typing import NamedTuple

import jax
import jax.numpy as jnp
from jax import lax
from jax.experimental import pallas as pl
from jax.experimental.pallas import tpu as pltpu

F32 = jnp.float32
BF16 = jnp.bfloat16

HEAD_DIM = 128
SUBLANES = 8
EPS = 1e-6
N_MOD = 6
POOL_WINDOWS = (2, 4, 8, 16)
HALO = 16
POOL_STATE = HALO - 1

V7X_VMEM_BYTES = 64 * 1024 * 1024
VMEM_LIMIT = V7X_VMEM_BYTES - 8 * 1024 * 1024

TILE_ROWS = 1024
TILE_COLS = 1024
TILE_QKV_ROWS = 1024
TILE_QKV_COLS = 512
ROW_SPLIT = 4
TILE_QKV_COLS_F32 = 512
QKV_SUBTILE = 256
FF_DOWN_COLS = 512
TILE_FFN_ROWS = 512
TILE_FF = 512
TILE_FF_F32 = 256
FF_SUBTILE = 256
TILE_POOL_ROWS = 512
TILE_Q = 256
LOGITS_AHEAD = 2
HEADS_PER_STEP = 8
TILE_CACHE = 512
TILE_ADA = 1024


def _cparams(sem):
    return pltpu.CompilerParams(dimension_semantics=sem, vmem_limit_bytes=VMEM_LIMIT)


def _normmod(x3, g, shift, scale):
    ms = jnp.mean(x3 * x3, axis=-1, keepdims=True)
    y = x3 * lax.rsqrt(ms + EPS) * g
    return y * (1.0 + scale) + shift


def _later_matrix(n):
    j = jnp.arange(n)[:, None]
    s = jnp.arange(n)[None, :]
    u = (j >= s).astype(BF16)
    return jnp.concatenate([u, u], axis=0)


LOG2_E = 1.4426950408889634
INV_LN2 = LOG2_E
SOFTPLUS2_CLAMP = 64.0
Q_PRESCALE = HEAD_DIM ** -0.5 * LOG2_E


def _sb_blocks(z2_fn, n_chains, uu_ref, carries, mask, pv_fn):
    z2s = [z2_fn(i) for i in range(min(LOGITS_AHEAD, n_chains))]
    sums, rowsums = [], []
    for i in range(n_chains):
        if i + LOGITS_AHEAD < n_chains:
            z2s.append(z2_fn(i + LOGITS_AHEAD))
        z2 = z2s[i]
        sp = jnp.maximum(z2, jnp.log(1.0 + jnp.exp2(jnp.minimum(z2, SOFTPLUS2_CLAMP))) * INV_LN2)
        if mask is not None:
            sp = jnp.where(mask, sp, 0.0)
        hi = sp.astype(BF16)
        lo = (sp - hi.astype(F32)).astype(BF16)
        total = jnp.dot(jnp.concatenate([hi, lo], axis=1), uu_ref[...], preferred_element_type=F32)
        sums.append(total)
        rowsums.append(total[:, :1])
    outs = []
    for i, (z2, total, rs, carry) in enumerate(zip(z2s, sums, rowsums, carries)):
        arg = z2 - total
        if carry is not None:
            arg = arg + carry
        w = jnp.exp2(arg)
        if mask is not None:
            w = jnp.where(mask, w, 0.0)
        outs.append((pv_fn(i, w.astype(BF16)), rs))
    return outs


def _ada_kernel(c_ref, w_ref, b_ref, o_ref):
    c = c_ref[...]
    s = (c * jax.nn.sigmoid(c)).astype(BF16)
    o_ref[...] = jnp.dot(s, w_ref[...].astype(BF16), preferred_element_type=F32) + b_ref[...]


def _ada_modulation(c_all, w_ada, b_ada):
    n_layers, d, _ = w_ada.shape
    bt = c_all.shape[0]
    tn = min(TILE_ADA, d)
    npc = d // tn
    out = pl.pallas_call(
        _ada_kernel,
        grid=(n_layers, N_MOD * npc),
        in_specs=[
            pl.BlockSpec((bt, d), lambda l, n: (0, 0)),
            pl.BlockSpec((None, d, tn), lambda l, n: (l, 0, n)),
            pl.BlockSpec((None, 1, tn), lambda l, n: (l, 0, n)),
        ],
        out_specs=pl.BlockSpec((None, None, bt, tn), lambda l, n: (l, n // npc, 0, n % npc)),
        out_shape=jax.ShapeDtypeStruct((n_layers, N_MOD, bt, d), F32),
        compiler_params=_cparams(("arbitrary", "arbitrary")),
        name="ada_modulation",
    )(c_all, w_ada, b_ada.reshape(n_layers, 1, N_MOD * d))
    return out.reshape(n_layers, N_MOD, bt, 1, d)


def _mod_spec(layer, which, bb, boff, d, ncols=None):
    if ncols is None:
        return pl.BlockSpec((None, None, bb, 1, d), lambda b, i, *_: (layer, which, boff // bb + b, 0, 0))
    return pl.BlockSpec((None, None, bb, 1, ncols), lambda b, i, n: (layer, which, boff // bb + b, 0, n))


def _head_ones(n):
    head = jnp.arange(n) // HEAD_DIM
    return (head[:, None] == head[None, :]).astype(BF16)


def _head_norm(y, g, ones_ref):
    ss = jnp.dot((y * y).astype(BF16), ones_ref[...], preferred_element_type=F32)
    gain = jnp.concatenate([g] * (y.shape[1] // HEAD_DIM), axis=1)
    return y * lax.rsqrt(ss * (1.0 / HEAD_DIM) + EPS) * gain


def _bf16_weight(w_ref, wb_ref):
    if wb_ref is None:
        return w_ref
    wb_ref[...] = w_ref[...].astype(BF16)
    return wb_ref


def _row_split(bb, tt):
    return ROW_SPLIT if bb == 1 and tt % (ROW_SPLIT * SUBLANES) == 0 else 1


def _row_split_specs(bb, tt, d):
    n = _row_split(bb, tt)
    return [pl.BlockSpec((bb, tt // n, d), lambda bi, i, *_, s=s: (bi, i * n + s, 0)) for s in range(n)]


def _normmod_rows(x_refs, g_ref, sh_ref, sc_ref, h_scr):
    rows = h_scr.shape[0] // len(x_refs)
    for s, x_ref in enumerate(x_refs):
        h = _normmod(x_ref[...], g_ref[...], sh_ref[...], sc_ref[...])
        h_scr[s * rows:(s + 1) * rows, :] = h.reshape(rows, h_scr.shape[1]).astype(BF16)


def _qkv_kernel(*refs, n_per, n_x):
    x_refs = refs[:n_x]
    (sh_ref, sc_ref, g_ref, w_ref, gq_ref, gk_ref, ones_ref,
     q_ref, k_ref, v_ref, kb_ref, vb_ref, *rest) = refs[n_x:]
    wb_ref, h_scr = rest if len(rest) == 2 else (None, *rest)
    j = pl.program_id(2)

    @pl.when(j == 0)
    def _():
        _normmod_rows(x_refs, g_ref, sh_ref, sc_ref, h_scr)

    bb, tt, tn = q_ref.shape
    sub = min(QKV_SUBTILE, tn)
    w_bf = _bf16_weight(w_ref, wb_ref)

    def project(c):
        return jnp.dot(h_scr[...], w_bf[:, c * sub:(c + 1) * sub], preferred_element_type=F32)

    def pipelined(epilogue):
        y = project(0)
        for c in range(tn // sub):
            nxt = project(c + 1) if (c + 1) * sub < tn else None
            epilogue(y, slice(c * sub, (c + 1) * sub))
            y = nxt

    def store_q(y, cols):
        q_ref[:, :, cols] = _head_norm(y, gq_ref[...] * Q_PRESCALE, ones_ref).astype(BF16).reshape(bb, tt, sub)

    def store_k(y, cols):
        k = _head_norm(y, gk_ref[...], ones_ref)
        k_ref[:, :, cols] = k.reshape(bb, tt, sub)
        kb_ref[:, :, cols] = k.astype(BF16).reshape(bb, tt, sub)

    def store_v(y, cols):
        v_ref[:, :, cols] = y.reshape(bb, tt, sub)
        vb_ref[:, :, cols] = y.astype(BF16).reshape(bb, tt, sub)

    pl.when(j < n_per)(lambda: pipelined(store_q))
    pl.when(jnp.logical_and(j >= n_per, j < 2 * n_per))(lambda: pipelined(store_k))
    pl.when(j >= 2 * n_per)(lambda: pipelined(store_v))


def _qkv_project(x, mod, boff, bb, tt, tn, g_mix, w_qkv, g_q, g_k):
    b, t, d = x.shape
    n_per = d // tn
    sub = min(QKV_SUBTILE, tn)
    emit_w = w_qkv.dtype != BF16
    grid = (b // bb, t // tt, 3 * n_per)
    q_spec = pl.BlockSpec((bb, tt, tn), lambda bi, i, j: (bi, i, jnp.minimum(j, n_per - 1)))
    k_spec = pl.BlockSpec((bb, tt, tn), lambda bi, i, j: (bi, i, jnp.clip(j - n_per, 0, n_per - 1)))
    v_spec = pl.BlockSpec((bb, tt, tn), lambda bi, i, j: (bi, i, jnp.maximum(j - 2 * n_per, 0)))
    w_spec = pl.BlockSpec((d, tn), lambda bi, i, j: (0, j))
    out_specs = [q_spec, k_spec, v_spec, k_spec, v_spec]
    out_shape = [
        jax.ShapeDtypeStruct((b, t, d), BF16),
        jax.ShapeDtypeStruct((b, t, d), F32),
        jax.ShapeDtypeStruct((b, t, d), F32),
        jax.ShapeDtypeStruct((b, t, d), BF16),
        jax.ShapeDtypeStruct((b, t, d), BF16),
    ]
    if emit_w:
        assert grid[:2] == (1, 1), "each weight tile must be visited exactly once to emit its bf16 copy"
        out_specs.append(w_spec)
        out_shape.append(jax.ShapeDtypeStruct(w_qkv.shape, BF16))
    x_specs = _row_split_specs(bb, tt, d)
    return pl.pallas_call(
        functools.partial(_qkv_kernel, n_per=n_per, n_x=len(x_specs)),
        grid=grid,
        in_specs=[
            *x_specs,
            _mod_spec(0, 0, bb, boff, d),
            _mod_spec(0, 1, bb, boff, d),
            pl.BlockSpec((1, d), lambda bi, i, j: (0, 0)),
            w_spec,
            pl.BlockSpec((1, HEAD_DIM), lambda bi, i, j: (0, 0)),
            pl.BlockSpec((1, HEAD_DIM), lambda bi, i, j: (0, 0)),
            pl.BlockSpec((sub, sub), lambda bi, i, j: (0, 0)),
        ],
        out_specs=out_specs,
        out_shape=out_shape,
        scratch_shapes=[pltpu.VMEM((bb * tt, d), BF16)],
        compiler_params=_cparams(("parallel", "parallel", "arbitrary")),
        name="qkv_project",
    )(*[x] * len(x_specs), mod, mod, g_mix, w_qkv, g_q, g_k, _head_ones(sub))


def _nt_dot(a, b):
    return lax.dot_general(a, b, (((1,), (1,)), ((), ())), preferred_element_type=F32)


def _attn_prompt_kernel(q_ref, k_ref, v_ref, uu_ref, o_ref, acc_scr, carry_scr, *, tq, n_par):
    t = q_ref.shape[0]
    row = lax.broadcasted_iota(jnp.int32, (tq, tq), 0)
    col = lax.broadcasted_iota(jnp.int32, (tq, tq), 1)
    causal = col < row
    heads = [slice(h * HEAD_DIM, (h + 1) * HEAD_DIM) for h in range(n_par)]

    def blocks(qrows, krows, first):
        carries = [None if first else carry_scr[i] for i in range(n_par)]
        res = _sb_blocks(lambda i: _nt_dot(q_ref[qrows, heads[i]], k_ref[krows, heads[i]]), n_par,
                         uu_ref, carries, causal if first else None,
                         lambda i, w: jnp.dot(w, v_ref[krows, heads[i]], preferred_element_type=F32))
        for i, (pv, rs) in enumerate(res):
            if first:
                acc_scr[i] = pv
                carry_scr[i] = -rs
            else:
                acc_scr[i] += pv
                carry_scr[i] -= rs

    def query_block(qi, _):
        qrows = pl.ds(pl.multiple_of(qi * tq, tq), tq)
        blocks(qrows, qrows, True)

        def body(step, _):
            blocks(qrows, pl.ds(pl.multiple_of((qi - 1 - step) * tq, tq), tq), False)
            return 0

        lax.fori_loop(0, qi, body, 0)
        for i, hc in enumerate(heads):
            o_ref[qrows, hc] = acc_scr[i].astype(o_ref.dtype)
        return 0

    lax.fori_loop(0, t // tq, query_block, 0)


def _attn_prompt(q, k, v):
    b, t, d = q.shape
    tq = min(TILE_Q, t)
    n_par = HEADS_PER_STEP
    width = n_par * HEAD_DIM
    spec = pl.BlockSpec((None, t, width), lambda bi, h: (bi, 0, h))
    return pl.pallas_call(
        functools.partial(_attn_prompt_kernel, tq=tq, n_par=n_par),
        grid=(b, d // width),
        in_specs=[spec, spec, spec, pl.BlockSpec((2 * tq, tq), lambda bi, h: (0, 0))],
        out_specs=spec,
        out_shape=jax.ShapeDtypeStruct((b, t, d), BF16),
        scratch_shapes=[pltpu.VMEM((n_par, tq, HEAD_DIM), F32), pltpu.VMEM((n_par, tq, 1), F32)],
        compiler_params=_cparams(("parallel", "parallel")),
        name="sb_attention_prompt",
    )(q, k, v, _later_matrix(tq))


def _attn_sample_kernel(q_ref, kn_ref, vn_ref, ck_ref, cv_ref, uu_ref, uun_ref, o_ref,
                        qbd_scr, knew_scr, vnew_scr, acc_scr, carry_scr, *, n_heads):
    j = pl.program_id(1)
    tnew, d = q_ref.shape
    tk = ck_ref.shape[0] // n_heads

    def all_heads(ref):
        return jnp.concatenate([ref[pl.ds(h, tk, stride=n_heads), :] for h in range(n_heads)],
                               axis=1).astype(BF16)

    @pl.when(j == 0)
    def _():
        q = q_ref[...]
        lane_head = lax.broadcasted_iota(jnp.int32, (tnew, d), 1) // HEAD_DIM
        for h in range(n_heads):
            qbd_scr[h * tnew:(h + 1) * tnew, :] = jnp.where(lane_head == h, q, jnp.zeros_like(q))
        knew_scr[...] = jnp.zeros_like(knew_scr)
        vnew_scr[...] = jnp.zeros_like(vnew_scr)
        knew_scr[0:tnew, :] = kn_ref[...].astype(BF16)
        vnew_scr[0:tnew, :] = vn_ref[...].astype(BF16)
        z2 = _nt_dot(qbd_scr[...], knew_scr[...])
        t_of_row = lax.broadcasted_iota(jnp.int32, z2.shape, 0) % tnew
        s_of_col = lax.broadcasted_iota(jnp.int32, z2.shape, 1)
        [(pv, rs)] = _sb_blocks(lambda i: z2, 1, uun_ref, [None], s_of_col < t_of_row,
                                lambda i, w: jnp.dot(w, vnew_scr[...], preferred_element_type=F32))
        acc_scr[...] = pv
        carry_scr[...] = -rs

    @pl.when(j > 0)
    def _():
        z2 = _nt_dot(qbd_scr[...], all_heads(ck_ref))
        vb = all_heads(cv_ref)
        [(pv, rs)] = _sb_blocks(lambda i: z2, 1, uu_ref, [carry_scr[...]], None,
                                lambda i, w: jnp.dot(w, vb, preferred_element_type=F32))
        acc_scr[...] += pv
        carry_scr[...] -= rs

    @pl.when(j == pl.num_programs(1) - 1)
    def _():
        for h in range(n_heads):
            cols = slice(h * HEAD_DIM, (h + 1) * HEAD_DIM)
            o_ref[:, cols] = acc_scr[h * tnew:(h + 1) * tnew, cols].astype(o_ref.dtype)


def _attn_sample(q, k_new, v_new, cache_k, cache_v):
    b, t, d = q.shape
    n_heads = d // HEAD_DIM
    p = cache_k.shape[1] // n_heads
    tk = min(TILE_CACHE, p)
    n_tiles = p // tk
    new_pad = 128
    row_spec = pl.BlockSpec((None, t, d), lambda bi, j: (bi, 0, 0))
    cache_spec = pl.BlockSpec((None, tk * n_heads, HEAD_DIM),
                              lambda bi, j: (bi, n_tiles - jnp.maximum(j, 1), 0))
    return pl.pallas_call(
        functools.partial(_attn_sample_kernel, n_heads=n_heads),
        grid=(b, n_tiles + 1),
        in_specs=[
            row_spec, row_spec, row_spec, cache_spec, cache_spec,
            pl.BlockSpec((2 * tk, tk), lambda bi, j: (0, 0)),
            pl.BlockSpec((2 * new_pad, new_pad), lambda bi, j: (0, 0)),
        ],
        out_specs=row_spec,
        out_shape=jax.ShapeDtypeStruct((b, t, d), BF16),
        scratch_shapes=[
            pltpu.VMEM((n_heads * t, d), BF16),
            pltpu.VMEM((new_pad, d), BF16),
            pltpu.VMEM((new_pad, d), BF16),
            pltpu.VMEM((n_heads * t, d), F32),
            pltpu.VMEM((n_heads * t, 1), F32),
        ],
        compiler_params=_cparams(("parallel", "arbitrary")),
        name="sb_attention_sample",
    )(q, k_new, v_new, cache_k, cache_v, _later_matrix(tk), _later_matrix(new_pad))


def _proj_res_kernel(a_ref, w_ref, x_ref, gate_ref, o_ref, wb_ref=None):
    bb, tt, k = a_ref.shape
    w_bf = _bf16_weight(w_ref, wb_ref)
    y = jnp.dot(a_ref[...].reshape(bb * tt, k), w_bf[...], preferred_element_type=F32)
    o_ref[...] = x_ref[...] + gate_ref[...] * y.reshape(o_ref.shape)


def _proj_residual(a, w, x, mod, layer, which, boff, bb, tt):
    b, t, k = a.shape
    d = w.shape[1]
    tn = min(TILE_COLS, d)
    grid = (b // bb, t // tt, d // tn)
    emit_w = w.dtype != BF16
    w_spec = pl.BlockSpec((k, tn), lambda bi, i, n: (0, n))
    out_specs = [pl.BlockSpec((bb, tt, tn), lambda bi, i, n: (bi, i, n))]
    out_shape = [jax.ShapeDtypeStruct((b, t, d), F32)]
    if emit_w:
        assert grid[:2] == (1, 1), "each weight tile must be visited exactly once to emit its bf16 copy"
        out_specs.append(w_spec)
        out_shape.append(jax.ShapeDtypeStruct(w.shape, BF16))
    outs = pl.pallas_call(
        _proj_res_kernel,
        grid=grid,
        in_specs=[
            pl.BlockSpec((bb, tt, k), lambda bi, i, n: (bi, i, 0)),
            w_spec,
            pl.BlockSpec((bb, tt, tn), lambda bi, i, n: (bi, i, n)),
            _mod_spec(layer, which, bb, boff, d, ncols=tn),
        ],
        out_specs=out_specs,
        out_shape=out_shape,
        compiler_params=_cparams(("parallel", "parallel", "arbitrary")),
        name="proj_residual",
    )(a, w, x, mod)
    return outs if emit_w else outs[0]


def _ffn_kernel(x_ref, sh_ref, sc_ref, gt_ref, g_ref, wg_ref, wu_ref, wd_ref, o_ref, *rest):
    wgb_ref, wub_ref, wdb_ref, h_scr = rest if len(rest) == 4 else (None, None, None, *rest)
    f = pl.program_id(2)

    @pl.when(f == 0)
    def _():
        h = _normmod(x_ref[...], g_ref[...], sh_ref[...], sc_ref[...])
        h_scr[...] = h.reshape(h_scr.shape).astype(BF16)
        o_ref[...] = jnp.zeros_like(o_ref)

    bb, tt, d = o_ref.shape
    tf = wg_ref.shape[1]
    sub = min(FF_SUBTILE, tf)
    dcols = min(FF_DOWN_COLS, d)
    wg_bf = _bf16_weight(wg_ref, wgb_ref)
    wu_bf = _bf16_weight(wu_ref, wub_ref)
    wd_bf = _bf16_weight(wd_ref, wdb_ref)

    def gate_up(c):
        cols = slice(c * sub, (c + 1) * sub)
        h = h_scr[...]
        return (jnp.dot(h, wg_bf[:, cols], preferred_element_type=F32),
                jnp.dot(h, wu_bf[:, cols], preferred_element_type=F32))

    gate, up = gate_up(0)
    for c in range(tf // sub):
        nxt = gate_up(c + 1) if (c + 1) * sub < tf else None
        a = (gate * jax.nn.sigmoid(gate) * up).astype(BF16)
        for n in range(d // dcols):
            cols = slice(n * dcols, (n + 1) * dcols)
            o_ref[:, :, cols] += jnp.dot(a, wd_bf[c * sub:(c + 1) * sub, cols],
                                         preferred_element_type=F32).reshape(bb, tt, dcols)
        if nxt is not None:
            gate, up = nxt

    @pl.when(f == pl.num_programs(2) - 1)
    def _():
        o_ref[...] = x_ref[...] + gt_ref[...] * o_ref[...]


def _layer_weight_spec(w, layer, block, index_map):
    if w.ndim == 2:
        return pl.BlockSpec(block, index_map)
    return pl.BlockSpec((None, *block), lambda *g: (layer, *index_map(*g)))


def _ffn(x, mod, layer, boff, bb, tt, tf, g_ffn, w_gate, w_up, w_down):
    b, t, d = x.shape
    ff = w_gate.shape[-1]
    grid = (b // bb, t // tt, ff // tf)
    emit_w = w_gate.dtype != BF16
    xspec = pl.BlockSpec((bb, tt, d), lambda bi, i, f: (bi, i, 0))
    up_block, up_map = (d, tf), lambda bi, i, f: (0, f)
    down_block, down_map = (tf, d), lambda bi, i, f: (f, 0)
    out_specs = [xspec]
    out_shape = [jax.ShapeDtypeStruct((b, t, d), F32)]
    if emit_w:
        assert grid[:2] == (1, 1), "each weight tile must be visited exactly once to emit its bf16 copy"
        out_specs += [pl.BlockSpec(up_block, up_map), pl.BlockSpec(up_block, up_map),
                      pl.BlockSpec(down_block, down_map)]
        out_shape += [jax.ShapeDtypeStruct((d, ff), BF16), jax.ShapeDtypeStruct((d, ff), BF16),
                      jax.ShapeDtypeStruct((ff, d), BF16)]
    outs = pl.pallas_call(
        _ffn_kernel,
        grid=grid,
        in_specs=[
            xspec,
            _mod_spec(layer, 3, bb, boff, d),
            _mod_spec(layer, 4, bb, boff, d),
            _mod_spec(layer, 5, bb, boff, d),
            pl.BlockSpec((1, d), lambda bi, i, f: (0, 0)),
            _layer_weight_spec(w_gate, layer, up_block, up_map),
            _layer_weight_spec(w_up, layer, up_block, up_map),
            _layer_weight_spec(w_down, layer, down_block, down_map),
        ],
        out_specs=out_specs,
        out_shape=out_shape,
        scratch_shapes=[pltpu.VMEM((bb * tt, d), BF16)],
        compiler_params=_cparams(("parallel", "parallel", "arbitrary")),
        name="swiglu_ffn",
    )(x, mod, mod, mod, g_ffn, w_gate, w_up, w_down)
    return outs if emit_w else outs[0]


def _pool_kernel(x_ref, halo_ref, sh_ref, sc_ref, gt_ref, g_ref, wp_ref, ps_ref, o_ref, tail_ref,
                 *, halo_is_x, n_hist):
    i = pl.program_id(1)
    bb, tt, d = x_ref.shape
    dg = d // len(POOL_WINDOWS)
    seg = HALO + tt
    x = x_ref[...]
    h = _normmod(x, g_ref[...], sh_ref[...], sc_ref[...])
    if halo_is_x:
        hist = _normmod(halo_ref[...], g_ref[...], sh_ref[...], sc_ref[...])
        hist = jnp.where(i > 0, hist, 0.0)
    else:
        hist = halo_ref[...]
    tail_ref[...] = h[:, tt - HALO:, :]

    s_all = jnp.concatenate([hist, h], axis=1).reshape(bb * seg, d)
    h2 = h.reshape(bb * tt, d)
    pos = i * tt + lax.broadcasted_iota(jnp.int32, (bb * tt, 1), 0) % tt
    outs = []
    for g, win in enumerate(POOL_WINDOWS):
        cols = slice(g * dg, (g + 1) * dg)
        s = s_all[:, cols]
        span = 1
        while span < win:
            s = s + pltpu.roll(s, span, 0)
            span *= 2
        s = s.reshape(bb, seg, dg)[:, HALO:, :].reshape(bb * tt, dg)
        count = jnp.minimum(n_hist + pos + 1, win).astype(F32)
        diff = (s / count - h2[:, cols]).astype(BF16)
        outs.append(jnp.dot(diff, wp_ref[g], preferred_element_type=F32))
    mix = (jnp.concatenate(outs, axis=1) * ps_ref[...]).reshape(bb, tt, d)
    o_ref[...] = x + gt_ref[...] * mix


def _pool_layer(x, halo, halo_is_x, n_hist, mod, layer, boff, bb, tt, g_mix, w_pool, pool_scale):
    b, t, d = x.shape
    dg = d // len(POOL_WINDOWS)
    if halo_is_x:
        per = tt // HALO
        halo_spec = pl.BlockSpec((bb, HALO, d), lambda bi, i: (bi, jnp.maximum(i * per - 1, 0), 0))
    else:
        halo_spec = pl.BlockSpec((bb, HALO, d), lambda bi, i: (bi, 0, 0))
    xspec = pl.BlockSpec((bb, tt, d), lambda bi, i: (bi, i, 0))
    return pl.pallas_call(
        functools.partial(_pool_kernel, halo_is_x=halo_is_x, n_hist=n_hist),
        grid=(b // bb, t // tt),
        in_specs=[
            xspec, halo_spec,
            _mod_spec(layer, 0, bb, boff, d),
            _mod_spec(layer, 1, bb, boff, d),
            _mod_spec(layer, 2, bb, boff, d),
            pl.BlockSpec((1, d), lambda bi, i: (0, 0)),
            pl.BlockSpec((len(POOL_WINDOWS), dg, dg), lambda bi, i: (0, 0, 0)),
            pl.BlockSpec((1, d), lambda bi, i: (0, 0)),
        ],
        out_specs=[xspec, pl.BlockSpec((bb, HALO, d), lambda bi, i: (bi, 0, 0))],
        out_shape=[jax.ShapeDtypeStruct((b, t, d), F32), jax.ShapeDtypeStruct((b, HALO, d), F32)],
        compiler_params=_cparams(("parallel", "arbitrary")),
        name="pool_mixer",
    )(x, halo, mod, mod, mod, g_mix, w_pool, pool_scale)


class _Tiles(NamedTuple):
    rows: int
    qkv_rows: int
    qkv_cols: int
    ffn_rows: int
    ffn_cols: int
    pool_rows: int


def _as_list(x):
    return list(x) if isinstance(x, (list, tuple)) else [x]


def _trunk(x, mod, boff, bb, tiles, caches, weights):
    (g_mix, g_ffn, w_qkv, g_q, g_k, w_o, w_pool, pool_scale, ffn_weights) = weights
    b, t, d = x.shape
    n_heads = d // HEAD_DIM
    emit_w = w_qkv.dtype != BF16
    q, k, v, k_bf, v_bf, *w_qkv_bf = _qkv_project(x, mod, boff, bb, tiles.qkv_rows, tiles.qkv_cols,
                                                  g_mix[0:1], w_qkv, g_q, g_k)
    if caches is None:
        o = _attn_prompt(q, k_bf, v_bf)
    else:
        o = _attn_sample(q, k, v, caches[0], caches[1])
    x, *w_o_bf = _as_list(_proj_residual(o, w_o, x, mod, 0, 2, boff, bb, tiles.rows))
    x, *ffn0_bf = _as_list(_ffn(x, mod, 0, boff, bb, tiles.ffn_rows, tiles.ffn_cols, g_ffn[0:1], *ffn_weights[0]))
    if caches is None:
        x, tail = _pool_layer(x, x, True, 0, mod, 1, boff, bb, tiles.pool_rows, g_mix[1:2], w_pool, pool_scale)
    else:
        hist = jnp.pad(caches[2], ((0, 0), (1, 0), (0, 0)))
        x, tail = _pool_layer(x, hist, False, caches[0].shape[1] // n_heads, mod, 1, boff, bb, tiles.pool_rows,
                              g_mix[1:2], w_pool, pool_scale)
    x, *ffn1_bf = _as_list(_ffn(x, mod, 1, boff, bb, tiles.ffn_rows, tiles.ffn_cols, g_ffn[1:2], *ffn_weights[1]))
    heads = (1, b, t, n_heads, HEAD_DIM)
    outs = (x, k.reshape(heads), v.reshape(heads), tail[None, :, 1:, :])
    if not emit_w:
        return outs, None
    return outs, (g_mix, g_ffn, w_qkv_bf[0], g_q, g_k, w_o_bf[0], w_pool, pool_scale, (ffn0_bf, ffn1_bf))


def kernel(x_prompt, x_sample, c_prompt, c_sample, cache_k, cache_v, state_pool, w_ada, b_ada, g_mix, g_ffn,
           w_qkv, g_q, g_k, w_o, w_pool, pool_scale, w_gate, w_up, w_down):
    bp, tp, d = x_prompt.shape
    bs, ts, _ = x_sample.shape
    past = cache_k.shape[2]
    ff = w_gate.shape[2]
    assert w_ada.shape[0] == 2 and cache_k.shape[0] == 1 and state_pool.shape[0] == 1 and w_qkv.shape[0] == 1
    assert state_pool.shape[2] == POOL_STATE and ts % HALO == 0 and d % (HEAD_DIM * len(POOL_WINDOWS)) == 0

    mod = _ada_modulation(jnp.concatenate([c_sample, c_prompt], axis=0), w_ada, b_ada)

    ffn_stacks = (w_gate, w_up, w_down)
    weights = (g_mix, g_ffn, w_qkv[0], g_q, g_k, w_o[0], w_pool[0].astype(BF16), pool_scale,
               (ffn_stacks, ffn_stacks))
    n_heads = d // HEAD_DIM
    assert n_heads % SUBLANES == 0
    caches = (cache_k.reshape(bs, past * n_heads, HEAD_DIM), cache_v.reshape(bs, past * n_heads, HEAD_DIM),
              state_pool[0])
    sample_tiles = _Tiles(rows=ts, qkv_rows=ts, qkv_cols=min(TILE_QKV_COLS_F32, d), ffn_rows=ts,
                          ffn_cols=min(TILE_FF_F32, ff), pool_rows=ts)
    (y_s, k_s, v_s, pool_s), weights_bf = _trunk(x_sample, mod, 0, bs, sample_tiles, caches, weights)
    y_s, weights_bf = lax.optimization_barrier((y_s, weights_bf))

    prompt_tiles = _Tiles(rows=min(TILE_ROWS, tp), qkv_rows=min(TILE_QKV_ROWS, tp), qkv_cols=min(TILE_QKV_COLS, d),
                          ffn_rows=min(TILE_FFN_ROWS, tp), ffn_cols=min(TILE_FF, ff),
                          pool_rows=min(TILE_POOL_ROWS, tp))
    (y_p, k_p, v_p, pool_p), _ = _trunk(x_prompt, mod, bs, 1, prompt_tiles, None,
                                        weights if weights_bf is None else weights_bf)
    return (y_p, y_s, k_p, v_p, pool_p, k_s, v_s, pool_s)
```

```python
import functools
from typing import NamedTuple

import jax
import jax.numpy as jnp
from jax import lax
from jax.experimental import pallas as pl
from jax.experimental.pallas import tpu as pltpu

F32 = jnp.float32
BF16 = jnp.bfloat16

HEAD_DIM = 128
SUBLANES = 8
EPS = 1e-6
N_MOD = 6
POOL_WINDOWS = (2, 4, 8, 16)
HALO = 16
POOL_STATE = HALO - 1

V7X_VMEM_BYTES = 64 * 1024 * 1024
VMEM_LIMIT = V7X_VMEM_BYTES - 8 * 1024 * 1024

TILE_ROWS = 1024
TILE_COLS = 1024
TILE_QKV_ROWS = 1024
TILE_QKV_COLS = 512
ROW_SPLIT = 4
TILE_QKV_COLS_F32 = 512
QKV_SUBTILE = 256
FF_DOWN_COLS = 512
TILE_FFN_ROWS = 512
TILE_FF = 512
TILE_FF_F32 = 256
FF_SUBTILE = 256
TILE_POOL_ROWS = 512
TILE_Q = 256
LOGITS_AHEAD = 2
HEADS_PER_STEP = 8
TILE_CACHE = 512
TILE_ADA = 1024


def _cparams(sem):
    return pltpu.CompilerParams(dimension_semantics=sem, vmem_limit_bytes=VMEM_LIMIT)


def _normmod(x3, g, shift, scale):
    ms = jnp.mean(x3 * x3, axis=-1, keepdims=True)
    y = x3 * lax.rsqrt(ms + EPS) * g
    return y * (1.0 + scale) + shift


def _later_matrix(n):
    j = jnp.arange(n)[:, None]
    s = jnp.arange(n)[None, :]
    u = (j >= s).astype(BF16)
    return jnp.concatenate([u, u], axis=0)


LOG2_E = 1.4426950408889634
INV_LN2 = LOG2_E
SOFTPLUS2_CLAMP = 64.0
Q_PRESCALE = HEAD_DIM ** -0.5 * LOG2_E


def _sb_blocks(z2_fn, n_chains, uu_ref, carries, mask, pv_fn):
    z2s = [z2_fn(i) for i in range(min(LOGITS_AHEAD, n_chains))]
    sums, rowsums = [], []
    for i in range(n_chains):
        if i + LOGITS_AHEAD < n_chains:
            z2s.append(z2_fn(i + LOGITS_AHEAD))
        z2 = z2s[i]
        sp = jnp.maximum(z2, jnp.log(1.0 + jnp.exp2(jnp.minimum(z2, SOFTPLUS2_CLAMP))) * INV_LN2)
        if mask is not None:
            sp = jnp.where(mask, sp, 0.0)
        hi = sp.astype(BF16)
        lo = (sp - hi.astype(F32)).astype(BF16)
        total = jnp.dot(jnp.concatenate([hi, lo], axis=1), uu_ref[...], preferred_element_type=F32)
        sums.append(total)
        rowsums.append(total[:, :1])
    outs = []
    for i, (z2, total, rs, carry) in enumerate(zip(z2s, sums, rowsums, carries)):
        arg = z2 - total
        if carry is not None:
            arg = arg + carry
        w = jnp.exp2(arg)
        if mask is not None:
            w = jnp.where(mask, w, 0.0)
        outs.append((pv_fn(i, w.astype(BF16)), rs))
    return outs


def _ada_kernel(c_ref, w_ref, b_ref, o_ref):
    c = c_ref[...]
    s = (c * jax.nn.sigmoid(c)).astype(BF16)
    o_ref[...] = jnp.dot(s, w_ref[...].astype(BF16), preferred_element_type=F32) + b_ref[...]


def _ada_modulation(c_all, w_ada, b_ada):
    n_layers, d, _ = w_ada.shape
    bt = c_all.shape[0]
    tn = min(TILE_ADA, d)
    npc = d // tn
    out = pl.pallas_call(
        _ada_kernel,
        grid=(n_layers, N_MOD * npc),
        in_specs=[
            pl.BlockSpec((bt, d), lambda l, n: (0, 0)),
            pl.BlockSpec((None, d, tn), lambda l, n: (l, 0, n)),
            pl.BlockSpec((None, 1, tn), lambda l, n: (l, 0, n)),
        ],
        out_specs=pl.BlockSpec((None, None, bt, tn), lambda l, n: (l, n // npc, 0, n % npc)),
        out_shape=jax.ShapeDtypeStruct((n_layers, N_MOD, bt, d), F32),
        compiler_params=_cparams(("arbitrary", "arbitrary")),
        name="ada_modulation",
    )(c_all, w_ada, b_ada.reshape(n_layers, 1, N_MOD * d))
    return out.reshape(n_layers, N_MOD, bt, 1, d)


def _mod_spec(layer, which, bb, boff, d, ncols=None):
    if ncols is None:
        return pl.BlockSpec((None, None, bb, 1, d), lambda b, i, *_: (layer, which, boff // bb + b, 0, 0))
    return pl.BlockSpec((None, None, bb, 1, ncols), lambda b, i, n: (layer, which, boff // bb + b, 0, n))


def _head_ones(n):
    head = jnp.arange(n) // HEAD_DIM
    return (head[:, None] == head[None, :]).astype(BF16)


def _head_norm(y, g, ones_ref):
    ss = jnp.dot((y * y).astype(BF16), ones_ref[...], preferred_element_type=F32)
    gain = jnp.concatenate([g] * (y.shape[1] // HEAD_DIM), axis=1)
    return y * lax.rsqrt(ss * (1.0 / HEAD_DIM) + EPS) * gain


def _bf16_weight(w_ref, wb_ref):
    if wb_ref is None:
        return w_ref
    wb_ref[...] = w_ref[...].astype(BF16)
    return wb_ref


def _row_split(bb, tt):
    return ROW_SPLIT if bb == 1 and tt % (ROW_SPLIT * SUBLANES) == 0 else 1


def _row_split_specs(bb, tt, d):
    n = _row_split(bb, tt)
    return [pl.BlockSpec((bb, tt // n, d), lambda bi, i, *_, s=s: (bi, i * n + s, 0)) for s in range(n)]


def _normmod_rows(x_refs, g_ref, sh_ref, sc_ref, h_scr):
    rows = h_scr.shape[0] // len(x_refs)
    for s, x_ref in enumerate(x_refs):
        h = _normmod(x_ref[...], g_ref[...], sh_ref[...], sc_ref[...])
        h_scr[s * rows:(s + 1) * rows, :] = h.reshape(rows, h_scr.shape[1]).astype(BF16)


def _qkv_kernel(*refs, n_per, n_x):
    x_refs = refs[:n_x]
    (sh_ref, sc_ref, g_ref, w_ref, gq_ref, gk_ref, ones_ref,
     q_ref, k_ref, v_ref, kb_ref, vb_ref, *rest) = refs[n_x:]
    wb_ref, h_scr = rest if len(rest) == 2 else (None, *rest)
    j = pl.program_id(2)

    @pl.when(j == 0)
    def _():
        _normmod_rows(x_refs, g_ref, sh_ref, sc_ref, h_scr)

    bb, tt, tn = q_ref.shape
    sub = min(QKV_SUBTILE, tn)
    w_bf = _bf16_weight(w_ref, wb_ref)

    def project(c):
        return jnp.dot(h_scr[...], w_bf[:, c * sub:(c + 1) * sub], preferred_element_type=F32)

    def pipelined(epilogue):
        y = project(0)
        for c in range(tn // sub):
            nxt = project(c + 1) if (c + 1) * sub < tn else None
            epilogue(y, slice(c * sub, (c + 1) * sub))
            y = nxt

    def store_q(y, cols):
        q_ref[:, :, cols] = _head_norm(y, gq_ref[...] * Q_PRESCALE, ones_ref).astype(BF16).reshape(bb, tt, sub)

    def store_k(y, cols):
        k = _head_norm(y, gk_ref[...], ones_ref)
        k_ref[:, :, cols] = k.reshape(bb, tt, sub)
        kb_ref[:, :, cols] = k.astype(BF16).reshape(bb, tt, sub)

    def store_v(y, cols):
        v_ref[:, :, cols] = y.reshape(bb, tt, sub)
        vb_ref[:, :, cols] = y.astype(BF16).reshape(bb, tt, sub)

    pl.when(j < n_per)(lambda: pipelined(store_q))
    pl.when(jnp.logical_and(j >= n_per, j < 2 * n_per))(lambda: pipelined(store_k))
    pl.when(j >= 2 * n_per)(lambda: pipelined(store_v))


def _qkv_project(x, mod, boff, bb, tt, tn, g_mix, w_qkv, g_q, g_k):
    b, t, d = x.shape
    n_per = d // tn
    sub = min(QKV_SUBTILE, tn)
    emit_w = w_qkv.dtype != BF16
    grid = (b // bb, t // tt, 3 * n_per)
    q_spec = pl.BlockSpec((bb, tt, tn), lambda bi, i, j: (bi, i, jnp.minimum(j, n_per - 1)))
    k_spec = pl.BlockSpec((bb, tt, tn), lambda bi, i, j: (bi, i, jnp.clip(j - n_per, 0, n_per - 1)))
    v_spec = pl.BlockSpec((bb, tt, tn), lambda bi, i, j: (bi, i, jnp.maximum(j - 2 * n_per, 0)))
    w_spec = pl.BlockSpec((d, tn), lambda bi, i, j: (0, j))
    out_specs = [q_spec, k_spec, v_spec, k_spec, v_spec]
    out_shape = [
        jax.ShapeDtypeStruct((b, t, d), BF16),
        jax.ShapeDtypeStruct((b, t, d), F32),
        jax.ShapeDtypeStruct((b, t, d), F32),
        jax.ShapeDtypeStruct((b, t, d), BF16),
        jax.ShapeDtypeStruct((b, t, d), BF16),
    ]
    if emit_w:
        assert grid[:2] == (1, 1), "each weight tile must be visited exactly once to emit its bf16 copy"
        out_specs.append(w_spec)
        out_shape.append(jax.ShapeDtypeStruct(w_qkv.shape, BF16))
    x_specs = _row_split_specs(bb, tt, d)
    return pl.pallas_call(
        functools.partial(_qkv_kernel, n_per=n_per, n_x=len(x_specs)),
        grid=grid,
        in_specs=[
            *x_specs,
            _mod_spec(0, 0, bb, boff, d),
            _mod_spec(0, 1, bb, boff, d),
            pl.BlockSpec((1, d), lambda bi, i, j: (0, 0)),
            w_spec,
            pl.BlockSpec((1, HEAD_DIM), lambda bi, i, j: (0, 0)),
            pl.BlockSpec((1, HEAD_DIM), lambda bi, i, j: (0, 0)),
            pl.BlockSpec((sub, sub), lambda bi, i, j: (0, 0)),
        ],
        out_specs=out_specs,
        out_shape=out_shape,
        scratch_shapes=[pltpu.VMEM((bb * tt, d), BF16)],
        compiler_params=_cparams(("parallel", "parallel", "arbitrary")),
        name="qkv_project",
    )(*[x] * len(x_specs), mod, mod, g_mix, w_qkv, g_q, g_k, _head_ones(sub))


def _nt_dot(a, b):
    return lax.dot_general(a, b, (((1,), (1,)), ((), ())), preferred_element_type=F32)


def _attn_prompt_kernel(q_ref, k_ref, v_ref, uu_ref, o_ref, acc_scr, carry_scr, *, tq, n_par):
    t = q_ref.shape[0]
    row = lax.broadcasted_iota(jnp.int32, (tq, tq), 0)
    col = lax.broadcasted_iota(jnp.int32, (tq, tq), 1)
    causal = col < row
    heads = [slice(h * HEAD_DIM, (h + 1) * HEAD_DIM) for h in range(n_par)]

    def blocks(qrows, krows, first):
        carries = [None if first else carry_scr[i] for i in range(n_par)]
        res = _sb_blocks(lambda i: _nt_dot(q_ref[qrows, heads[i]], k_ref[krows, heads[i]]), n_par,
                         uu_ref, carries, causal if first else None,
                         lambda i, w: jnp.dot(w, v_ref[krows, heads[i]], preferred_element_type=F32))
        for i, (pv, rs) in enumerate(res):
            if first:
                acc_scr[i] = pv
                carry_scr[i] = -rs
            else:
                acc_scr[i] += pv
                carry_scr[i] -= rs

    def query_block(qi, _):
        qrows = pl.ds(pl.multiple_of(qi * tq, tq), tq)
        blocks(qrows, qrows, True)

        def body(step, _):
            blocks(qrows, pl.ds(pl.multiple_of((qi - 1 - step) * tq, tq), tq), False)
            return 0

        lax.fori_loop(0, qi, body, 0)
        for i, hc in enumerate(heads):
            o_ref[qrows, hc] = acc_scr[i].astype(o_ref.dtype)
        return 0

    lax.fori_loop(0, t // tq, query_block, 0)


def _attn_prompt(q, k, v):
    b, t, d = q.shape
    tq = min(TILE_Q, t)
    n_par = HEADS_PER_STEP
    width = n_par * HEAD_DIM
    spec = pl.BlockSpec((None, t, width), lambda bi, h: (bi, 0, h))
    return pl.pallas_call(
        functools.partial(_attn_prompt_kernel, tq=tq, n_par=n_par),
        grid=(b, d // width),
        in_specs=[spec, spec, spec, pl.BlockSpec((2 * tq, tq), lambda bi, h: (0, 0))],
        out_specs=spec,
        out_shape=jax.ShapeDtypeStruct((b, t, d), BF16),
        scratch_shapes=[pltpu.VMEM((n_par, tq, HEAD_DIM), F32), pltpu.VMEM((n_par, tq, 1), F32)],
        compiler_params=_cparams(("parallel", "parallel")),
        name="sb_attention_prompt",
    )(q, k, v, _later_matrix(tq))


def _head_rows(group_ref, i):
    tk = group_ref.shape[0]
    return group_ref.reshape(tk * SUBLANES, HEAD_DIM)[pl.ds(i, tk, stride=SUBLANES), :]


def _attn_sample_kernel(q_ref, kn_ref, vn_ref, *refs, n_heads):
    n_groups = n_heads // SUBLANES
    ck_refs, cv_refs = refs[:n_groups], refs[n_groups:2 * n_groups]
    uu_ref, uun_ref, o_ref, qbd_scr, knew_scr, vnew_scr, acc_scr, carry_scr = refs[2 * n_groups:]
    j = pl.program_id(1)
    tnew, d = q_ref.shape

    def all_heads(group_refs):
        return jnp.concatenate([_head_rows(group_refs[h // SUBLANES], h % SUBLANES) for h in range(n_heads)],
                               axis=1).astype(BF16)

    @pl.when(j == 0)
    def _():
        q = q_ref[...]
        lane_head = lax.broadcasted_iota(jnp.int32, (tnew, d), 1) // HEAD_DIM
        for h in range(n_heads):
            qbd_scr[h * tnew:(h + 1) * tnew, :] = jnp.where(lane_head == h, q, jnp.zeros_like(q))
        knew_scr[...] = jnp.zeros_like(knew_scr)
        vnew_scr[...] = jnp.zeros_like(vnew_scr)
        knew_scr[0:tnew, :] = kn_ref[...].astype(BF16)
        vnew_scr[0:tnew, :] = vn_ref[...].astype(BF16)
        z2 = _nt_dot(qbd_scr[...], knew_scr[...])
        t_of_row = lax.broadcasted_iota(jnp.int32, z2.shape, 0) % tnew
        s_of_col = lax.broadcasted_iota(jnp.int32, z2.shape, 1)
        [(pv, rs)] = _sb_blocks(lambda i: z2, 1, uun_ref, [None], s_of_col < t_of_row,
                                lambda i, w: jnp.dot(w, vnew_scr[...], preferred_element_type=F32))
        acc_scr[...] = pv
        carry_scr[...] = -rs

    @pl.when(j > 0)
    def _():
        z2 = _nt_dot(qbd_scr[...], all_heads(ck_refs))
        vb = all_heads(cv_refs)
        [(pv, rs)] = _sb_blocks(lambda i: z2, 1, uu_ref, [carry_scr[...]], None,
                                lambda i, w: jnp.dot(w, vb, preferred_element_type=F32))
        acc_scr[...] += pv
        carry_scr[...] -= rs

    @pl.when(j == pl.num_programs(1) - 1)
    def _():
        for h in range(n_heads):
            cols = slice(h * HEAD_DIM, (h + 1) * HEAD_DIM)
            o_ref[:, cols] = acc_scr[h * tnew:(h + 1) * tnew, cols].astype(o_ref.dtype)


def _attn_sample(q, k_new, v_new, cache_k, cache_v):
    b, t, d = q.shape
    n_heads = d // HEAD_DIM
    p, n_groups = cache_k.shape[1:3]
    tk = min(TILE_CACHE, p)
    n_tiles = p // tk
    new_pad = 128
    row_spec = pl.BlockSpec((None, t, d), lambda bi, j: (bi, 0, 0))
    cache_specs = [pl.BlockSpec((None, tk, None, SUBLANES, HEAD_DIM),
                                lambda bi, j, g=g: (bi, n_tiles - jnp.maximum(j, 1), g, 0, 0))
                   for g in range(n_groups)]
    return pl.pallas_call(
        functools.partial(_attn_sample_kernel, n_heads=n_heads),
        grid=(b, n_tiles + 1),
        in_specs=[
            row_spec, row_spec, row_spec, *cache_specs, *cache_specs,
            pl.BlockSpec((2 * tk, tk), lambda bi, j: (0, 0)),
            pl.BlockSpec((2 * new_pad, new_pad), lambda bi, j: (0, 0)),
        ],
        out_specs=row_spec,
        out_shape=jax.ShapeDtypeStruct((b, t, d), BF16),
        scratch_shapes=[
            pltpu.VMEM((n_heads * t, d), BF16),
            pltpu.VMEM((new_pad, d), BF16),
            pltpu.VMEM((new_pad, d), BF16),
            pltpu.VMEM((n_heads * t, d), F32),
            pltpu.VMEM((n_heads * t, 1), F32),
        ],
        compiler_params=_cparams(("parallel", "arbitrary")),
        name="sb_attention_sample",
    )(q, k_new, v_new, *[cache_k] * n_groups, *[cache_v] * n_groups, _later_matrix(tk), _later_matrix(new_pad))


def _proj_res_kernel(a_ref, w_ref, x_ref, gate_ref, o_ref, wb_ref=None):
    bb, tt, k = a_ref.shape
    w_bf = _bf16_weight(w_ref, wb_ref)
    y = jnp.dot(a_ref[...].reshape(bb * tt, k), w_bf[...], preferred_element_type=F32)
    o_ref[...] = x_ref[...] + gate_ref[...] * y.reshape(o_ref.shape)


def _proj_residual(a, w, x, mod, layer, which, boff, bb, tt):
    b, t, k = a.shape
    d = w.shape[1]
    tn = min(TILE_COLS, d)
    grid = (b // bb, t // tt, d // tn)
    emit_w = w.dtype != BF16
    w_spec = pl.BlockSpec((k, tn), lambda bi, i, n: (0, n))
    out_specs = [pl.BlockSpec((bb, tt, tn), lambda bi, i, n: (bi, i, n))]
    out_shape = [jax.ShapeDtypeStruct((b, t, d), F32)]
    if emit_w:
        assert grid[:2] == (1, 1), "each weight tile must be visited exactly once to emit its bf16 copy"
        out_specs.append(w_spec)
        out_shape.append(jax.ShapeDtypeStruct(w.shape, BF16))
    outs = pl.pallas_call(
        _proj_res_kernel,
        grid=grid,
        in_specs=[
            pl.BlockSpec((bb, tt, k), lambda bi, i, n: (bi, i, 0)),
            w_spec,
            pl.BlockSpec((bb, tt, tn), lambda bi, i, n: (bi, i, n)),
            _mod_spec(layer, which, bb, boff, d, ncols=tn),
        ],
        out_specs=out_specs,
        out_shape=out_shape,
        compiler_params=_cparams(("parallel", "parallel", "arbitrary")),
        name="proj_residual",
    )(a, w, x, mod)
    return outs if emit_w else outs[0]


def _ffn_kernel(x_ref, sh_ref, sc_ref, gt_ref, g_ref, wg_ref, wu_ref, wd_ref, o_ref, *rest):
    wgb_ref, wub_ref, wdb_ref, h_scr = rest if len(rest) == 4 else (None, None, None, *rest)
    f = pl.program_id(2)

    @pl.when(f == 0)
    def _():
        h = _normmod(x_ref[...], g_ref[...], sh_ref[...], sc_ref[...])
        h_scr[...] = h.reshape(h_scr.shape).astype(BF16)
        o_ref[...] = jnp.zeros_like(o_ref)

    bb, tt, d = o_ref.shape
    tf = wg_ref.shape[1]
    sub = min(FF_SUBTILE, tf)
    dcols = min(FF_DOWN_COLS, d)
    wg_bf = _bf16_weight(wg_ref, wgb_ref)
    wu_bf = _bf16_weight(wu_ref, wub_ref)
    wd_bf = _bf16_weight(wd_ref, wdb_ref)

    def gate_up(c):
        cols = slice(c * sub, (c + 1) * sub)
        h = h_scr[...]
        return (jnp.dot(h, wg_bf[:, cols], preferred_element_type=F32),
                jnp.dot(h, wu_bf[:, cols], preferred_element_type=F32))

    gate, up = gate_up(0)
    for c in range(tf // sub):
        nxt = gate_up(c + 1) if (c + 1) * sub < tf else None
        a = (gate * jax.nn.sigmoid(gate) * up).astype(BF16)
        for n in range(d // dcols):
            cols = slice(n * dcols, (n + 1) * dcols)
            o_ref[:, :, cols] += jnp.dot(a, wd_bf[c * sub:(c + 1) * sub, cols],
                                         preferred_element_type=F32).reshape(bb, tt, dcols)
        if nxt is not None:
            gate, up = nxt

    @pl.when(f == pl.num_programs(2) - 1)
    def _():
        o_ref[...] = x_ref[...] + gt_ref[...] * o_ref[...]


def _layer_weight_spec(w, layer, block, index_map):
    if w.ndim == 2:
        return pl.BlockSpec(block, index_map)
    return pl.BlockSpec((None, *block), lambda *g: (layer, *index_map(*g)))


def _ffn(x, mod, layer, boff, bb, tt, tf, g_ffn, w_gate, w_up, w_down):
    b, t, d = x.shape
    ff = w_gate.shape[-1]
    grid = (b // bb, t // tt, ff // tf)
    emit_w = w_gate.dtype != BF16
    xspec = pl.BlockSpec((bb, tt, d), lambda bi, i, f: (bi, i, 0))
    up_block, up_map = (d, tf), lambda bi, i, f: (0, f)
    down_block, down_map = (tf, d), lambda bi, i, f: (f, 0)
    out_specs = [xspec]
    out_shape = [jax.ShapeDtypeStruct((b, t, d), F32)]
    if emit_w:
        assert grid[:2] == (1, 1), "each weight tile must be visited exactly once to emit its bf16 copy"
        out_specs += [pl.BlockSpec(up_block, up_map), pl.BlockSpec(up_block, up_map),
                      pl.BlockSpec(down_block, down_map)]
        out_shape += [jax.ShapeDtypeStruct((d, ff), BF16), jax.ShapeDtypeStruct((d, ff), BF16),
                      jax.ShapeDtypeStruct((ff, d), BF16)]
    outs = pl.pallas_call(
        _ffn_kernel,
        grid=grid,
        in_specs=[
            xspec,
            _mod_spec(layer, 3, bb, boff, d),
            _mod_spec(layer, 4, bb, boff, d),
            _mod_spec(layer, 5, bb, boff, d),
            pl.BlockSpec((1, d), lambda bi, i, f: (0, 0)),
            _layer_weight_spec(w_gate, layer, up_block, up_map),
            _layer_weight_spec(w_up, layer, up_block, up_map),
            _layer_weight_spec(w_down, layer, down_block, down_map),
        ],
        out_specs=out_specs,
        out_shape=out_shape,
        scratch_shapes=[pltpu.VMEM((bb * tt, d), BF16)],
        compiler_params=_cparams(("parallel", "parallel", "arbitrary")),
        name="swiglu_ffn",
    )(x, mod, mod, mod, g_ffn, w_gate, w_up, w_down)
    return outs if emit_w else outs[0]


def _pool_kernel(x_ref, halo_ref, sh_ref, sc_ref, gt_ref, g_ref, wp_ref, ps_ref, o_ref, tail_ref,
                 *, halo_is_x, n_hist):
    i = pl.program_id(1)
    bb, tt, d = x_ref.shape
    dg = d // len(POOL_WINDOWS)
    seg = HALO + tt
    x = x_ref[...]
    h = _normmod(x, g_ref[...], sh_ref[...], sc_ref[...])
    if halo_is_x:
        hist = _normmod(halo_ref[...], g_ref[...], sh_ref[...], sc_ref[...])
        hist = jnp.where(i > 0, hist, 0.0)
    else:
        hist = halo_ref[...]
    tail_ref[...] = h[:, tt - HALO:, :]

    s_all = jnp.concatenate([hist, h], axis=1).reshape(bb * seg, d)
    h2 = h.reshape(bb * tt, d)
    pos = i * tt + lax.broadcasted_iota(jnp.int32, (bb * tt, 1), 0) % tt
    outs = []
    for g, win in enumerate(POOL_WINDOWS):
        cols = slice(g * dg, (g + 1) * dg)
        s = s_all[:, cols]
        span = 1
        while span < win:
            s = s + pltpu.roll(s, span, 0)
            span *= 2
        s = s.reshape(bb, seg, dg)[:, HALO:, :].reshape(bb * tt, dg)
        count = jnp.minimum(n_hist + pos + 1, win).astype(F32)
        diff = (s / count - h2[:, cols]).astype(BF16)
        outs.append(jnp.dot(diff, wp_ref[g], preferred_element_type=F32))
    mix = (jnp.concatenate(outs, axis=1) * ps_ref[...]).reshape(bb, tt, d)
    o_ref[...] = x + gt_ref[...] * mix


def _pool_layer(x, halo, halo_is_x, n_hist, mod, layer, boff, bb, tt, g_mix, w_pool, pool_scale):
    b, t, d = x.shape
    dg = d // len(POOL_WINDOWS)
    if halo_is_x:
        per = tt // HALO
        halo_spec = pl.BlockSpec((bb, HALO, d), lambda bi, i: (bi, jnp.maximum(i * per - 1, 0), 0))
    else:
        halo_spec = pl.BlockSpec((bb, HALO, d), lambda bi, i: (bi, 0, 0))
    xspec = pl.BlockSpec((bb, tt, d), lambda bi, i: (bi, i, 0))
    return pl.pallas_call(
        functools.partial(_pool_kernel, halo_is_x=halo_is_x, n_hist=n_hist),
        grid=(b // bb, t // tt),
        in_specs=[
            xspec, halo_spec,
            _mod_spec(layer, 0, bb, boff, d),
            _mod_spec(layer, 1, bb, boff, d),
            _mod_spec(layer, 2, bb, boff, d),
            pl.BlockSpec((1, d), lambda bi, i: (0, 0)),
            pl.BlockSpec((len(POOL_WINDOWS), dg, dg), lambda bi, i: (0, 0, 0)),
            pl.BlockSpec((1, d), lambda bi, i: (0, 0)),
        ],
        out_specs=[xspec, pl.BlockSpec((bb, HALO, d), lambda bi, i: (bi, 0, 0))],
        out_shape=[jax.ShapeDtypeStruct((b, t, d), F32), jax.ShapeDtypeStruct((b, HALO, d), F32)],
        compiler_params=_cparams(("parallel", "arbitrary")),
        name="pool_mixer",
    )(x, halo, mod, mod, mod, g_mix, w_pool, pool_scale)


class _Tiles(NamedTuple):
    rows: int
    qkv_rows: int
    qkv_cols: int
    ffn_rows: int
    ffn_cols: int
    pool_rows: int


def _as_list(x):
    return list(x) if isinstance(x, (list, tuple)) else [x]


def _trunk(x, mod, boff, bb, tiles, caches, weights):
    (g_mix, g_ffn, w_qkv, g_q, g_k, w_o, w_pool, pool_scale, ffn_weights) = weights
    b, t, d = x.shape
    n_heads = d // HEAD_DIM
    emit_w = w_qkv.dtype != BF16
    q, k, v, k_bf, v_bf, *w_qkv_bf = _qkv_project(x, mod, boff, bb, tiles.qkv_rows, tiles.qkv_cols,
                                                  g_mix[0:1], w_qkv, g_q, g_k)
    if caches is None:
        o = _attn_prompt(q, k_bf, v_bf)
    else:
        o = _attn_sample(q, k, v, caches[0], caches[1])
    x, *w_o_bf = _as_list(_proj_residual(o, w_o, x, mod, 0, 2, boff, bb, tiles.rows))
    x, *ffn0_bf = _as_list(_ffn(x, mod, 0, boff, bb, tiles.ffn_rows, tiles.ffn_cols, g_ffn[0:1], *ffn_weights[0]))
    if caches is None:
        x, tail = _pool_layer(x, x, True, 0, mod, 1, boff, bb, tiles.pool_rows, g_mix[1:2], w_pool, pool_scale)
    else:
        hist = jnp.pad(caches[2], ((0, 0), (1, 0), (0, 0)))
        x, tail = _pool_layer(x, hist, False, caches[0].shape[1], mod, 1, boff, bb, tiles.pool_rows,
                              g_mix[1:2], w_pool, pool_scale)
    x, *ffn1_bf = _as_list(_ffn(x, mod, 1, boff, bb, tiles.ffn_rows, tiles.ffn_cols, g_ffn[1:2], *ffn_weights[1]))
    heads = (1, b, t, n_heads, HEAD_DIM)
    outs = (x, k.reshape(heads), v.reshape(heads), tail[None, :, 1:, :])
    if not emit_w:
        return outs, None
    return outs, (g_mix, g_ffn, w_qkv_bf[0], g_q, g_k, w_o_bf[0], w_pool, pool_scale, (ffn0_bf, ffn1_bf))


def kernel(x_prompt, x_sample, c_prompt, c_sample, cache_k, cache_v, state_pool, w_ada, b_ada, g_mix, g_ffn,
           w_qkv, g_q, g_k, w_o, w_pool, pool_scale, w_gate, w_up, w_down):
    bp, tp, d = x_prompt.shape
    bs, ts, _ = x_sample.shape
    past = cache_k.shape[2]
    ff = w_gate.shape[2]
    assert w_ada.shape[0] == 2 and cache_k.shape[0] == 1 and state_pool.shape[0] == 1 and w_qkv.shape[0] == 1
    assert state_pool.shape[2] == POOL_STATE and ts % HALO == 0 and d % (HEAD_DIM * len(POOL_WINDOWS)) == 0

    mod = _ada_modulation(jnp.concatenate([c_sample, c_prompt], axis=0), w_ada, b_ada)

    ffn_stacks = (w_gate, w_up, w_down)
    weights = (g_mix, g_ffn, w_qkv[0], g_q, g_k, w_o[0], w_pool[0].astype(BF16), pool_scale,
               (ffn_stacks, ffn_stacks))
    assert (d // HEAD_DIM) % SUBLANES == 0
    cache_shape = (bs, past, d // HEAD_DIM // SUBLANES, SUBLANES, HEAD_DIM)
    caches = (cache_k.reshape(cache_shape), cache_v.reshape(cache_shape), state_pool[0])
    sample_tiles = _Tiles(rows=ts, qkv_rows=ts, qkv_cols=min(TILE_QKV_COLS_F32, d), ffn_rows=ts,
                          ffn_cols=min(TILE_FF_F32, ff), pool_rows=ts)
    (y_s, k_s, v_s, pool_s), weights_bf = _trunk(x_sample, mod, 0, bs, sample_tiles, caches, weights)
    y_s, weights_bf = lax.optimization_barrier((y_s, weights_bf))

    prompt_tiles = _Tiles(rows=min(TILE_ROWS, tp), qkv_rows=min(TILE_QKV_ROWS, tp), qkv_cols=min(TILE_QKV_COLS, d),
                          ffn_rows=min(TILE_FFN_ROWS, tp), ffn_cols=min(TILE_FF, ff),
                          pool_rows=min(TILE_POOL_ROWS, tp))
    (y_p, k_p, v_p, pool_p), _ = _trunk(x_prompt, mod, bs, 1, prompt_tiles, None,
                                        weights if weights_bf is None else weights_bf)
    return (y_p, y_s, k_p, v_p, pool_p, k_s, v_s, pool_s)
```

```python
import functools
from typing import NamedTuple

import jax
import jax.numpy as jnp
from jax import lax
from jax.experimental import pallas as pl
from jax.experimental.pallas import tpu as pltpu

F32 = jnp.float32
BF16 = jnp.bfloat16

HEAD_DIM = 128
SUBLANES = 8
EPS = 1e-6
N_MOD = 6
POOL_WINDOWS = (2, 4, 8, 16)
HALO = 16
POOL_STATE = HALO - 1

V7X_VMEM_BYTES = 64 * 1024 * 1024
VMEM_LIMIT = V7X_VMEM_BYTES - 8 * 1024 * 1024

TILE_ROWS = 1024
TILE_COLS = 1024
TILE_QKV_ROWS = 1024
TILE_QKV_COLS = 512
ROW_SPLIT = 4
TILE_QKV_COLS_F32 = 512
QKV_SUBTILE = 256
FF_DOWN_COLS = 512
TILE_FFN_ROWS = 512
TILE_FF = 512
TILE_FF_F32 = 256
FF_SUBTILE = 256
TILE_POOL_ROWS = 512
TILE_Q = 256
LOGITS_AHEAD = 2
HEADS_PER_STEP = 8
TILE_CACHE = 512
CACHE_SPLIT = 2
TILE_ADA = 1024


def _cparams(sem):
    return pltpu.CompilerParams(dimension_semantics=sem, vmem_limit_bytes=VMEM_LIMIT)


def _normmod(x3, g, shift, scale):
    ms = jnp.mean(x3 * x3, axis=-1, keepdims=True)
    y = x3 * lax.rsqrt(ms + EPS) * g
    return y * (1.0 + scale) + shift


def _later_matrix(n):
    j = jnp.arange(n)[:, None]
    s = jnp.arange(n)[None, :]
    u = (j >= s).astype(BF16)
    return jnp.concatenate([u, u], axis=0)


LOG2_E = 1.4426950408889634
INV_LN2 = LOG2_E
SOFTPLUS2_CLAMP = 64.0
Q_PRESCALE = HEAD_DIM ** -0.5 * LOG2_E


def _sb_blocks(z2_fn, n_chains, uu_ref, carries, mask, pv_fn):
    z2s = [z2_fn(i) for i in range(min(LOGITS_AHEAD, n_chains))]
    sums, rowsums = [], []
    for i in range(n_chains):
        if i + LOGITS_AHEAD < n_chains:
            z2s.append(z2_fn(i + LOGITS_AHEAD))
        z2 = z2s[i]
        sp = jnp.maximum(z2, jnp.log(1.0 + jnp.exp2(jnp.minimum(z2, SOFTPLUS2_CLAMP))) * INV_LN2)
        if mask is not None:
            sp = jnp.where(mask, sp, 0.0)
        hi = sp.astype(BF16)
        lo = (sp - hi.astype(F32)).astype(BF16)
        total = jnp.dot(jnp.concatenate([hi, lo], axis=1), uu_ref[...], preferred_element_type=F32)
        sums.append(total)
        rowsums.append(total[:, :1])
    outs = []
    for i, (z2, total, rs, carry) in enumerate(zip(z2s, sums, rowsums, carries)):
        arg = z2 - total
        if carry is not None:
            arg = arg + carry
        w = jnp.exp2(arg)
        if mask is not None:
            w = jnp.where(mask, w, 0.0)
        outs.append((pv_fn(i, w.astype(BF16)), rs))
    return outs


def _ada_kernel(c_ref, w_ref, b_ref, o_ref):
    c = c_ref[...]
    s = (c * jax.nn.sigmoid(c)).astype(BF16)
    o_ref[...] = jnp.dot(s, w_ref[...].astype(BF16), preferred_element_type=F32) + b_ref[...]


def _ada_modulation(c_all, w_ada, b_ada):
    n_layers, d, _ = w_ada.shape
    bt = c_all.shape[0]
    tn = min(TILE_ADA, d)
    npc = d // tn
    out = pl.pallas_call(
        _ada_kernel,
        grid=(n_layers, N_MOD * npc),
        in_specs=[
            pl.BlockSpec((bt, d), lambda l, n: (0, 0)),
            pl.BlockSpec((None, d, tn), lambda l, n: (l, 0, n)),
            pl.BlockSpec((None, 1, tn), lambda l, n: (l, 0, n)),
        ],
        out_specs=pl.BlockSpec((None, None, bt, tn), lambda l, n: (l, n // npc, 0, n % npc)),
        out_shape=jax.ShapeDtypeStruct((n_layers, N_MOD, bt, d), F32),
        compiler_params=_cparams(("arbitrary", "arbitrary")),
        name="ada_modulation",
    )(c_all, w_ada, b_ada.reshape(n_layers, 1, N_MOD * d))
    return out.reshape(n_layers, N_MOD, bt, 1, d)


def _mod_spec(layer, which, bb, boff, d, ncols=None):
    if ncols is None:
        return pl.BlockSpec((None, None, bb, 1, d), lambda b, i, *_: (layer, which, boff // bb + b, 0, 0))
    return pl.BlockSpec((None, None, bb, 1, ncols), lambda b, i, n: (layer, which, boff // bb + b, 0, n))


def _head_ones(n):
    head = jnp.arange(n) // HEAD_DIM
    return (head[:, None] == head[None, :]).astype(BF16)


def _head_norm(y, g, ones_ref):
    ss = jnp.dot((y * y).astype(BF16), ones_ref[...], preferred_element_type=F32)
    gain = jnp.concatenate([g] * (y.shape[1] // HEAD_DIM), axis=1)
    return y * lax.rsqrt(ss * (1.0 / HEAD_DIM) + EPS) * gain


def _bf16_weight(w_ref, wb_ref):
    if wb_ref is None:
        return w_ref
    wb_ref[...] = w_ref[...].astype(BF16)
    return wb_ref


def _row_split(bb, tt):
    return ROW_SPLIT if bb == 1 and tt % (ROW_SPLIT * SUBLANES) == 0 else 1


def _row_split_specs(bb, tt, d):
    n = _row_split(bb, tt)
    return [pl.BlockSpec((bb, tt // n, d), lambda bi, i, *_, s=s: (bi, i * n + s, 0)) for s in range(n)]


def _normmod_rows(x_refs, g_ref, sh_ref, sc_ref, h_scr):
    rows = h_scr.shape[0] // len(x_refs)
    for s, x_ref in enumerate(x_refs):
        h = _normmod(x_ref[...], g_ref[...], sh_ref[...], sc_ref[...])
        h_scr[s * rows:(s + 1) * rows, :] = h.reshape(rows, h_scr.shape[1]).astype(BF16)


def _qkv_kernel(*refs, n_per, n_x):
    x_refs = refs[:n_x]
    (sh_ref, sc_ref, g_ref, w_ref, gq_ref, gk_ref, ones_ref,
     q_ref, k_ref, v_ref, kb_ref, vb_ref, *rest) = refs[n_x:]
    wb_ref, h_scr = rest if len(rest) == 2 else (None, *rest)
    j = pl.program_id(2)

    @pl.when(j == 0)
    def _():
        _normmod_rows(x_refs, g_ref, sh_ref, sc_ref, h_scr)

    bb, tt, tn = q_ref.shape
    sub = min(QKV_SUBTILE, tn)
    w_bf = _bf16_weight(w_ref, wb_ref)

    def project(c):
        return jnp.dot(h_scr[...], w_bf[:, c * sub:(c + 1) * sub], preferred_element_type=F32)

    def pipelined(epilogue):
        y = project(0)
        for c in range(tn // sub):
            nxt = project(c + 1) if (c + 1) * sub < tn else None
            epilogue(y, slice(c * sub, (c + 1) * sub))
            y = nxt

    def store_q(y, cols):
        q_ref[:, :, cols] = _head_norm(y, gq_ref[...] * Q_PRESCALE, ones_ref).astype(BF16).reshape(bb, tt, sub)

    def store_k(y, cols):
        k = _head_norm(y, gk_ref[...], ones_ref)
        k_ref[:, :, cols] = k.reshape(bb, tt, sub)
        kb_ref[:, :, cols] = k.astype(BF16).reshape(bb, tt, sub)

    def store_v(y, cols):
        v_ref[:, :, cols] = y.reshape(bb, tt, sub)
        vb_ref[:, :, cols] = y.astype(BF16).reshape(bb, tt, sub)

    pl.when(j < n_per)(lambda: pipelined(store_q))
    pl.when(jnp.logical_and(j >= n_per, j < 2 * n_per))(lambda: pipelined(store_k))
    pl.when(j >= 2 * n_per)(lambda: pipelined(store_v))


def _qkv_project(x, mod, boff, bb, tt, tn, g_mix, w_qkv, g_q, g_k):
    b, t, d = x.shape
    n_per = d // tn
    sub = min(QKV_SUBTILE, tn)
    emit_w = w_qkv.dtype != BF16
    grid = (b // bb, t // tt, 3 * n_per)
    q_spec = pl.BlockSpec((bb, tt, tn), lambda bi, i, j: (bi, i, jnp.minimum(j, n_per - 1)))
    k_spec = pl.BlockSpec((bb, tt, tn), lambda bi, i, j: (bi, i, jnp.clip(j - n_per, 0, n_per - 1)))
    v_spec = pl.BlockSpec((bb, tt, tn), lambda bi, i, j: (bi, i, jnp.maximum(j - 2 * n_per, 0)))
    w_spec = pl.BlockSpec((d, tn), lambda bi, i, j: (0, j))
    out_specs = [q_spec, k_spec, v_spec, k_spec, v_spec]
    out_shape = [
        jax.ShapeDtypeStruct((b, t, d), BF16),
        jax.ShapeDtypeStruct((b, t, d), F32),
        jax.ShapeDtypeStruct((b, t, d), F32),
        jax.ShapeDtypeStruct((b, t, d), BF16),
        jax.ShapeDtypeStruct((b, t, d), BF16),
    ]
    if emit_w:
        assert grid[:2] == (1, 1), "each weight tile must be visited exactly once to emit its bf16 copy"
        out_specs.append(w_spec)
        out_shape.append(jax.ShapeDtypeStruct(w_qkv.shape, BF16))
    x_specs = _row_split_specs(bb, tt, d)
    return pl.pallas_call(
        functools.partial(_qkv_kernel, n_per=n_per, n_x=len(x_specs)),
        grid=grid,
        in_specs=[
            *x_specs,
            _mod_spec(0, 0, bb, boff, d),
            _mod_spec(0, 1, bb, boff, d),
            pl.BlockSpec((1, d), lambda bi, i, j: (0, 0)),
            w_spec,
            pl.BlockSpec((1, HEAD_DIM), lambda bi, i, j: (0, 0)),
            pl.BlockSpec((1, HEAD_DIM), lambda bi, i, j: (0, 0)),
            pl.BlockSpec((sub, sub), lambda bi, i, j: (0, 0)),
        ],
        out_specs=out_specs,
        out_shape=out_shape,
        scratch_shapes=[pltpu.VMEM((bb * tt, d), BF16)],
        compiler_params=_cparams(("parallel", "parallel", "arbitrary")),
        name="qkv_project",
    )(*[x] * len(x_specs), mod, mod, g_mix, w_qkv, g_q, g_k, _head_ones(sub))


def _nt_dot(a, b):
    return lax.dot_general(a, b, (((1,), (1,)), ((), ())), preferred_element_type=F32)


def _attn_prompt_kernel(q_ref, k_ref, v_ref, uu_ref, o_ref, acc_scr, carry_scr, *, tq, n_par):
    t = q_ref.shape[0]
    row = lax.broadcasted_iota(jnp.int32, (tq, tq), 0)
    col = lax.broadcasted_iota(jnp.int32, (tq, tq), 1)
    causal = col < row
    heads = [slice(h * HEAD_DIM, (h + 1) * HEAD_DIM) for h in range(n_par)]

    def blocks(qrows, krows, first):
        carries = [None if first else carry_scr[i] for i in range(n_par)]
        res = _sb_blocks(lambda i: _nt_dot(q_ref[qrows, heads[i]], k_ref[krows, heads[i]]), n_par,
                         uu_ref, carries, causal if first else None,
                         lambda i, w: jnp.dot(w, v_ref[krows, heads[i]], preferred_element_type=F32))
        for i, (pv, rs) in enumerate(res):
            if first:
                acc_scr[i] = pv
                carry_scr[i] = -rs
            else:
                acc_scr[i] += pv
                carry_scr[i] -= rs

    def query_block(qi, _):
        qrows = pl.ds(pl.multiple_of(qi * tq, tq), tq)
        blocks(qrows, qrows, True)

        def body(step, _):
            blocks(qrows, pl.ds(pl.multiple_of((qi - 1 - step) * tq, tq), tq), False)
            return 0

        lax.fori_loop(0, qi, body, 0)
        for i, hc in enumerate(heads):
            o_ref[qrows, hc] = acc_scr[i].astype(o_ref.dtype)
        return 0

    lax.fori_loop(0, t // tq, query_block, 0)


def _attn_prompt(q, k, v):
    b, t, d = q.shape
    tq = min(TILE_Q, t)
    n_par = HEADS_PER_STEP
    width = n_par * HEAD_DIM
    spec = pl.BlockSpec((None, t, width), lambda bi, h: (bi, 0, h))
    return pl.pallas_call(
        functools.partial(_attn_prompt_kernel, tq=tq, n_par=n_par),
        grid=(b, d // width),
        in_specs=[spec, spec, spec, pl.BlockSpec((2 * tq, tq), lambda bi, h: (0, 0))],
        out_specs=spec,
        out_shape=jax.ShapeDtypeStruct((b, t, d), BF16),
        scratch_shapes=[pltpu.VMEM((n_par, tq, HEAD_DIM), F32), pltpu.VMEM((n_par, tq, 1), F32)],
        compiler_params=_cparams(("parallel", "parallel")),
        name="sb_attention_prompt",
    )(q, k, v, _later_matrix(tq))


def _head_rows(group_ref, i):
    tk = group_ref.shape[0]
    return group_ref.reshape(tk * SUBLANES, HEAD_DIM)[pl.ds(i, tk, stride=SUBLANES), :]


def _attn_sample_kernel(q_ref, kn_ref, vn_ref, *refs, n_heads):
    n_groups = n_heads // SUBLANES
    n_parts = n_groups * CACHE_SPLIT
    ck_refs, cv_refs = refs[:n_parts], refs[n_parts:2 * n_parts]
    uu_ref, uun_ref, o_ref, qbd_scr, knew_scr, vnew_scr, acc_scr, carry_scr = refs[2 * n_parts:]
    j = pl.program_id(1)
    tnew, d = q_ref.shape

    def all_heads(part_refs):
        def head(h):
            g, i = divmod(h, SUBLANES)
            return jnp.concatenate([_head_rows(part_refs[s * n_groups + g], i) for s in range(CACHE_SPLIT)],
                                   axis=0)
        return jnp.concatenate([head(h) for h in range(n_heads)], axis=1).astype(BF16)

    @pl.when(j == 0)
    def _():
        q = q_ref[...]
        lane_head = lax.broadcasted_iota(jnp.int32, (tnew, d), 1) // HEAD_DIM
        for h in range(n_heads):
            qbd_scr[h * tnew:(h + 1) * tnew, :] = jnp.where(lane_head == h, q, jnp.zeros_like(q))
        knew_scr[...] = jnp.zeros_like(knew_scr)
        vnew_scr[...] = jnp.zeros_like(vnew_scr)
        knew_scr[0:tnew, :] = kn_ref[...].astype(BF16)
        vnew_scr[0:tnew, :] = vn_ref[...].astype(BF16)
        z2 = _nt_dot(qbd_scr[...], knew_scr[...])
        t_of_row = lax.broadcasted_iota(jnp.int32, z2.shape, 0) % tnew
        s_of_col = lax.broadcasted_iota(jnp.int32, z2.shape, 1)
        [(pv, rs)] = _sb_blocks(lambda i: z2, 1, uun_ref, [None], s_of_col < t_of_row,
                                lambda i, w: jnp.dot(w, vnew_scr[...], preferred_element_type=F32))
        acc_scr[...] = pv
        carry_scr[...] = -rs

    @pl.when(j > 0)
    def _():
        z2 = _nt_dot(qbd_scr[...], all_heads(ck_refs))
        vb = all_heads(cv_refs)
        [(pv, rs)] = _sb_blocks(lambda i: z2, 1, uu_ref, [carry_scr[...]], None,
                                lambda i, w: jnp.dot(w, vb, preferred_element_type=F32))
        acc_scr[...] += pv
        carry_scr[...] -= rs

    @pl.when(j == pl.num_programs(1) - 1)
    def _():
        for h in range(n_heads):
            cols = slice(h * HEAD_DIM, (h + 1) * HEAD_DIM)
            o_ref[:, cols] = acc_scr[h * tnew:(h + 1) * tnew, cols].astype(o_ref.dtype)


def _attn_sample(q, k_new, v_new, cache_k, cache_v):
    b, t, d = q.shape
    n_heads = d // HEAD_DIM
    p, n_groups = cache_k.shape[1:3]
    tk = min(TILE_CACHE, p)
    n_tiles = p // tk
    new_pad = 128
    row_spec = pl.BlockSpec((None, t, d), lambda bi, j: (bi, 0, 0))
    part = tk // CACHE_SPLIT
    cache_specs = [pl.BlockSpec((None, part, None, SUBLANES, HEAD_DIM),
                                lambda bi, j, s=s, g=g: (bi, (n_tiles - jnp.maximum(j, 1)) * CACHE_SPLIT + s, g, 0, 0))
                   for s in range(CACHE_SPLIT) for g in range(n_groups)]
    return pl.pallas_call(
        functools.partial(_attn_sample_kernel, n_heads=n_heads),
        grid=(b, n_tiles + 1),
        in_specs=[
            row_spec, row_spec, row_spec, *cache_specs, *cache_specs,
            pl.BlockSpec((2 * tk, tk), lambda bi, j: (0, 0)),
            pl.BlockSpec((2 * new_pad, new_pad), lambda bi, j: (0, 0)),
        ],
        out_specs=row_spec,
        out_shape=jax.ShapeDtypeStruct((b, t, d), BF16),
        scratch_shapes=[
            pltpu.VMEM((n_heads * t, d), BF16),
            pltpu.VMEM((new_pad, d), BF16),
            pltpu.VMEM((new_pad, d), BF16),
            pltpu.VMEM((n_heads * t, d), F32),
            pltpu.VMEM((n_heads * t, 1), F32),
        ],
        compiler_params=_cparams(("parallel", "arbitrary")),
        name="sb_attention_sample",
    )(q, k_new, v_new, *[cache_k] * len(cache_specs), *[cache_v] * len(cache_specs),
      _later_matrix(tk), _later_matrix(new_pad))


def _proj_res_kernel(a_ref, w_ref, x_ref, gate_ref, o_ref, wb_ref=None):
    bb, tt, k = a_ref.shape
    w_bf = _bf16_weight(w_ref, wb_ref)
    y = jnp.dot(a_ref[...].reshape(bb * tt, k), w_bf[...], preferred_element_type=F32)
    o_ref[...] = x_ref[...] + gate_ref[...] * y.reshape(o_ref.shape)


def _proj_residual(a, w, x, mod, layer, which, boff, bb, tt):
    b, t, k = a.shape
    d = w.shape[1]
    tn = min(TILE_COLS, d)
    grid = (b // bb, t // tt, d // tn)
    emit_w = w.dtype != BF16
    w_spec = pl.BlockSpec((k, tn), lambda bi, i, n: (0, n))
    out_specs = [pl.BlockSpec((bb, tt, tn), lambda bi, i, n: (bi, i, n))]
    out_shape = [jax.ShapeDtypeStruct((b, t, d), F32)]
    if emit_w:
        assert grid[:2] == (1, 1), "each weight tile must be visited exactly once to emit its bf16 copy"
        out_specs.append(w_spec)
        out_shape.append(jax.ShapeDtypeStruct(w.shape, BF16))
    outs = pl.pallas_call(
        _proj_res_kernel,
        grid=grid,
        in_specs=[
            pl.BlockSpec((bb, tt, k), lambda bi, i, n: (bi, i, 0)),
            w_spec,
            pl.BlockSpec((bb, tt, tn), lambda bi, i, n: (bi, i, n)),
            _mod_spec(layer, which, bb, boff, d, ncols=tn),
        ],
        out_specs=out_specs,
        out_shape=out_shape,
        compiler_params=_cparams(("parallel", "parallel", "arbitrary")),
        name="proj_residual",
    )(a, w, x, mod)
    return outs if emit_w else outs[0]


def _ffn_kernel(x_ref, sh_ref, sc_ref, gt_ref, g_ref, wg_ref, wu_ref, wd_ref, o_ref, *rest):
    wgb_ref, wub_ref, wdb_ref, h_scr = rest if len(rest) == 4 else (None, None, None, *rest)
    f = pl.program_id(2)

    @pl.when(f == 0)
    def _():
        h = _normmod(x_ref[...], g_ref[...], sh_ref[...], sc_ref[...])
        h_scr[...] = h.reshape(h_scr.shape).astype(BF16)
        o_ref[...] = jnp.zeros_like(o_ref)

    bb, tt, d = o_ref.shape
    tf = wg_ref.shape[1]
    sub = min(FF_SUBTILE, tf)
    dcols = min(FF_DOWN_COLS, d)
    wg_bf = _bf16_weight(wg_ref, wgb_ref)
    wu_bf = _bf16_weight(wu_ref, wub_ref)
    wd_bf = _bf16_weight(wd_ref, wdb_ref)

    def gate_up(c):
        cols = slice(c * sub, (c + 1) * sub)
        h = h_scr[...]
        return (jnp.dot(h, wg_bf[:, cols], preferred_element_type=F32),
                jnp.dot(h, wu_bf[:, cols], preferred_element_type=F32))

    gate, up = gate_up(0)
    for c in range(tf // sub):
        nxt = gate_up(c + 1) if (c + 1) * sub < tf else None
        a = (gate * jax.nn.sigmoid(gate) * up).astype(BF16)
        for n in range(d // dcols):
            cols = slice(n * dcols, (n + 1) * dcols)
            o_ref[:, :, cols] += jnp.dot(a, wd_bf[c * sub:(c + 1) * sub, cols],
                                         preferred_element_type=F32).reshape(bb, tt, dcols)
        if nxt is not None:
            gate, up = nxt

    @pl.when(f == pl.num_programs(2) - 1)
    def _():
        o_ref[...] = x_ref[...] + gt_ref[...] * o_ref[...]


def _layer_weight_spec(w, layer, block, index_map):
    if w.ndim == 2:
        return pl.BlockSpec(block, index_map)
    return pl.BlockSpec((None, *block), lambda *g: (layer, *index_map(*g)))


def _ffn(x, mod, layer, boff, bb, tt, tf, g_ffn, w_gate, w_up, w_down):
    b, t, d = x.shape
    ff = w_gate.shape[-1]
    grid = (b // bb, t // tt, ff // tf)
    emit_w = w_gate.dtype != BF16
    xspec = pl.BlockSpec((bb, tt, d), lambda bi, i, f: (bi, i, 0))
    up_block, up_map = (d, tf), lambda bi, i, f: (0, f)
    down_block, down_map = (tf, d), lambda bi, i, f: (f, 0)
    out_specs = [xspec]
    out_shape = [jax.ShapeDtypeStruct((b, t, d), F32)]
    if emit_w:
        assert grid[:2] == (1, 1), "each weight tile must be visited exactly once to emit its bf16 copy"
        out_specs += [pl.BlockSpec(up_block, up_map), pl.BlockSpec(up_block, up_map),
                      pl.BlockSpec(down_block, down_map)]
        out_shape += [jax.ShapeDtypeStruct((d, ff), BF16), jax.ShapeDtypeStruct((d, ff), BF16),
                      jax.ShapeDtypeStruct((ff, d), BF16)]
    outs = pl.pallas_call(
        _ffn_kernel,
        grid=grid,
        in_specs=[
            xspec,
            _mod_spec(layer, 3, bb, boff, d),
            _mod_spec(layer, 4, bb, boff, d),
            _mod_spec(layer, 5, bb, boff, d),
            pl.BlockSpec((1, d), lambda bi, i, f: (0, 0)),
            _layer_weight_spec(w_gate, layer, up_block, up_map),
            _layer_weight_spec(w_up, layer, up_block, up_map),
            _layer_weight_spec(w_down, layer, down_block, down_map),
        ],
        out_specs=out_specs,
        out_shape=out_shape,
        scratch_shapes=[pltpu.VMEM((bb * tt, d), BF16)],
        compiler_params=_cparams(("parallel", "parallel", "arbitrary")),
        name="swiglu_ffn",
    )(x, mod, mod, mod, g_ffn, w_gate, w_up, w_down)
    return outs if emit_w else outs[0]


def _pool_kernel(x_ref, halo_ref, sh_ref, sc_ref, gt_ref, g_ref, wp_ref, ps_ref, o_ref, tail_ref,
                 *, halo_is_x, n_hist):
    i = pl.program_id(1)
    bb, tt, d = x_ref.shape
    dg = d // len(POOL_WINDOWS)
    seg = HALO + tt
    x = x_ref[...]
    h = _normmod(x, g_ref[...], sh_ref[...], sc_ref[...])
    if halo_is_x:
        hist = _normmod(halo_ref[...], g_ref[...], sh_ref[...], sc_ref[...])
        hist = jnp.where(i > 0, hist, 0.0)
    else:
        hist = halo_ref[...]
    tail_ref[...] = h[:, tt - HALO:, :]

    s_all = jnp.concatenate([hist, h], axis=1).reshape(bb * seg, d)
    h2 = h.reshape(bb * tt, d)
    pos = i * tt + lax.broadcasted_iota(jnp.int32, (bb * tt, 1), 0) % tt
    outs = []
    for g, win in enumerate(POOL_WINDOWS):
        cols = slice(g * dg, (g + 1) * dg)
        s = s_all[:, cols]
        span = 1
        while span < win:
            s = s + pltpu.roll(s, span, 0)
            span *= 2
        s = s.reshape(bb, seg, dg)[:, HALO:, :].reshape(bb * tt, dg)
        count = jnp.minimum(n_hist + pos + 1, win).astype(F32)
        diff = (s / count - h2[:, cols]).astype(BF16)
        outs.append(jnp.dot(diff, wp_ref[g], preferred_element_type=F32))
    mix = (jnp.concatenate(outs, axis=1) * ps_ref[...]).reshape(bb, tt, d)
    o_ref[...] = x + gt_ref[...] * mix


def _pool_layer(x, halo, halo_is_x, n_hist, mod, layer, boff, bb, tt, g_mix, w_pool, pool_scale):
    b, t, d = x.shape
    dg = d // len(POOL_WINDOWS)
    if halo_is_x:
        per = tt // HALO
        halo_spec = pl.BlockSpec((bb, HALO, d), lambda bi, i: (bi, jnp.maximum(i * per - 1, 0), 0))
    else:
        halo_spec = pl.BlockSpec((bb, HALO, d), lambda bi, i: (bi, 0, 0))
    xspec = pl.BlockSpec((bb, tt, d), lambda bi, i: (bi, i, 0))
    return pl.pallas_call(
        functools.partial(_pool_kernel, halo_is_x=halo_is_x, n_hist=n_hist),
        grid=(b // bb, t // tt),
        in_specs=[
            xspec, halo_spec,
            _mod_spec(layer, 0, bb, boff, d),
            _mod_spec(layer, 1, bb, boff, d),
            _mod_spec(layer, 2, bb, boff, d),
            pl.BlockSpec((1, d), lambda bi, i: (0, 0)),
            pl.BlockSpec((len(POOL_WINDOWS), dg, dg), lambda bi, i: (0, 0, 0)),
            pl.BlockSpec((1, d), lambda bi, i: (0, 0)),
        ],
        out_specs=[xspec, pl.BlockSpec((bb, HALO, d), lambda bi, i: (bi, 0, 0))],
        out_shape=[jax.ShapeDtypeStruct((b, t, d), F32), jax.ShapeDtypeStruct((b, HALO, d), F32)],
        compiler_params=_cparams(("parallel", "arbitrary")),
        name="pool_mixer",
    )(x, halo, mod, mod, mod, g_mix, w_pool, pool_scale)


class _Tiles(NamedTuple):
    rows: int
    qkv_rows: int
    qkv_cols: int
    ffn_rows: int
    ffn_cols: int
    pool_rows: int


def _as_list(x):
    return list(x) if isinstance(x, (list, tuple)) else [x]


def _trunk(x, mod, boff, bb, tiles, caches, weights):
    (g_mix, g_ffn, w_qkv, g_q, g_k, w_o, w_pool, pool_scale, ffn_weights) = weights
    b, t, d = x.shape
    n_heads = d // HEAD_DIM
    emit_w = w_qkv.dtype != BF16
    q, k, v, k_bf, v_bf, *w_qkv_bf = _qkv_project(x, mod, boff, bb, tiles.qkv_rows, tiles.qkv_cols,
                                                  g_mix[0:1], w_qkv, g_q, g_k)
    if caches is None:
        o = _attn_prompt(q, k_bf, v_bf)
    else:
        o = _attn_sample(q, k, v, caches[0], caches[1])
    x, *w_o_bf = _as_list(_proj_residual(o, w_o, x, mod, 0, 2, boff, bb, tiles.rows))
    x, *ffn0_bf = _as_list(_ffn(x, mod, 0, boff, bb, tiles.ffn_rows, tiles.ffn_cols, g_ffn[0:1], *ffn_weights[0]))
    if caches is None:
        x, tail = _pool_layer(x, x, True, 0, mod, 1, boff, bb, tiles.pool_rows, g_mix[1:2], w_pool, pool_scale)
    else:
        hist = jnp.pad(caches[2], ((0, 0), (1, 0), (0, 0)))
        x, tail = _pool_layer(x, hist, False, caches[0].shape[1], mod, 1, boff, bb, tiles.pool_rows,
                              g_mix[1:2], w_pool, pool_scale)
    x, *ffn1_bf = _as_list(_ffn(x, mod, 1, boff, bb, tiles.ffn_rows, tiles.ffn_cols, g_ffn[1:2], *ffn_weights[1]))
    heads = (1, b, t, n_heads, HEAD_DIM)
    outs = (x, k.reshape(heads), v.reshape(heads), tail[None, :, 1:, :])
    if not emit_w:
        return outs, None
    return outs, (g_mix, g_ffn, w_qkv_bf[0], g_q, g_k, w_o_bf[0], w_pool, pool_scale, (ffn0_bf, ffn1_bf))


def kernel(x_prompt, x_sample, c_prompt, c_sample, cache_k, cache_v, state_pool, w_ada, b_ada, g_mix, g_ffn,
           w_qkv, g_q, g_k, w_o, w_pool, pool_scale, w_gate, w_up, w_down):
    bp, tp, d = x_prompt.shape
    bs, ts, _ = x_sample.shape
    past = cache_k.shape[2]
    ff = w_gate.shape[2]
    assert w_ada.shape[0] == 2 and cache_k.shape[0] == 1 and state_pool.shape[0] == 1 and w_qkv.shape[0] == 1
    assert state_pool.shape[2] == POOL_STATE and ts % HALO == 0 and d % (HEAD_DIM * len(POOL_WINDOWS)) == 0

    mod = _ada_modulation(jnp.concatenate([c_sample, c_prompt], axis=0), w_ada, b_ada)

    ffn_stacks = (w_gate, w_up, w_down)
    weights = (g_mix, g_ffn, w_qkv[0], g_q, g_k, w_o[0], w_pool[0].astype(BF16), pool_scale,
               (ffn_stacks, ffn_stacks))
    assert (d // HEAD_DIM) % SUBLANES == 0
    cache_shape = (bs, past, d // HEAD_DIM // SUBLANES, SUBLANES, HEAD_DIM)
    caches = (cache_k.reshape(cache_shape), cache_v.reshape(cache_shape), state_pool[0])
    sample_tiles = _Tiles(rows=ts, qkv_rows=ts, qkv_cols=min(TILE_QKV_COLS_F32, d), ffn_rows=ts,
                          ffn_cols=min(TILE_FF_F32, ff), pool_rows=ts)
    (y_s, k_s, v_s, pool_s), weights_bf = _trunk(x_sample, mod, 0, bs, sample_tiles, caches, weights)
    y_s, weights_bf = lax.optimization_barrier((y_s, weights_bf))

    prompt_tiles = _Tiles(rows=min(TILE_ROWS, tp), qkv_rows=min(TILE_QKV_ROWS, tp), qkv_cols=min(TILE_QKV_COLS, d),
                          ffn_rows=min(TILE_FFN_ROWS, tp), ffn_cols=min(TILE_FF, ff),
                          pool_rows=min(TILE_POOL_ROWS, tp))
    (y_p, k_p, v_p, pool_p), _ = _trunk(x_prompt, mod, bs, 1, prompt_tiles, None,
                                        weights if weights_bf is None else weights_bf)
    return (y_p, y_s, k_p, v_p, pool_p, k_s, v_s, pool_s)
```
